```python
import jax, jax.numpy as jnp
from jax import lax
import numpy as np

D_MODEL = 2048
BATCH = 4
SEQ = 4096
DEPTH = 2
DEC_BATCH = 2
DEC_SEQ = 16384
PAST_LEN = 128

HEAD_DIM = 128
N_Q_HEADS = 16
N_KV_HEADS = 4
Q_PER_KV = N_Q_HEADS // N_KV_HEADS
ATT_WIDTH = N_Q_HEADS * HEAD_DIM
KV_WIDTH = N_KV_HEADS * HEAD_DIM
WINDOW = 128
BLOCK = 128
ROPE_THETA = 10000.0
GM_WIDTH = 2048
GM_GROUPS = 16
GM_GROUP_DIM = GM_WIDTH // GM_GROUPS
GM_CHUNK = 128
MEM_TOKENS = 256
MEM_HEADS = 4
MEM_HEAD_DIM = 128
MEM_WIDTH = MEM_HEADS * MEM_HEAD_DIM
PEER_HEADS = 8
PEER_QDIM = 256
PEER_HALF = PEER_QDIM // 2
N_KEYS = 128
N_EXPERTS = N_KEYS * N_KEYS
PEER_TOPK = 16
PEER_TOKEN_CHUNK = 128
NORM_EPS = 1e-6
NEG_INF = -1e30

IN_SIZES = (ATT_WIDTH, KV_WIDTH, KV_WIDTH, GM_WIDTH, GM_WIDTH, D_MODEL, D_MODEL)
IN_WIDTH = sum(IN_SIZES)
IN_OFFSETS = tuple(int(o) for o in np.cumsum(IN_SIZES)[:-1])

kernel_name = "hybrid_bidir_swa_gmlp_peer_encoder"


def rms_norm(x, g):
    xf = x.astype(jnp.float32)
    y = xf * lax.rsqrt(jnp.mean(xf * xf, axis=-1, keepdims=True) + NORM_EPS)
    return (y * g.astype(jnp.float32)).astype(x.dtype)


def layer_norm(x, g, b):
    xf = x.astype(jnp.float32)
    mu = jnp.mean(xf, axis=-1, keepdims=True)
    xc = xf - mu
    y = xc * lax.rsqrt(jnp.mean(xc * xc, axis=-1, keepdims=True) + NORM_EPS)
    return (y * g.astype(jnp.float32) + b.astype(jnp.float32)).astype(x.dtype)


def rotary(x, pos):
    half = HEAD_DIM // 2
    inv = ROPE_THETA ** (-jnp.arange(half, dtype=jnp.float32) / half)
    ang = pos[:, None] * inv[None, :]
    cos = jnp.cos(ang)[None, :, None, :]
    sin = jnp.sin(ang)[None, :, None, :]
    xf = x.astype(jnp.float32)
    x1, x2 = xf[..., :half], xf[..., half:]
    return jnp.concatenate([x1 * cos - x2 * sin, x1 * sin + x2 * cos], axis=-1).astype(x.dtype)


def banded_sink_attention(q, k, v, sink):
    B, S = q.shape[0], q.shape[1]
    nb = S // BLOCK
    qb = q.reshape(B, nb, BLOCK, N_KV_HEADS, Q_PER_KV, HEAD_DIM)

    def band(t):
        tp = jnp.pad(t, ((0, 0), (BLOCK, BLOCK), (0, 0), (0, 0)))
        tp = tp.reshape(B, nb + 2, BLOCK, N_KV_HEADS, HEAD_DIM)
        return jnp.concatenate([tp[:, :-2], tp[:, 1:-1], tp[:, 2:]], axis=2)

    kb, vb = band(k), band(v)
    s = jnp.einsum('bnqhgd,bnkhd->bnhgqk', qb, kb).astype(jnp.float32) * (HEAD_DIM ** -0.5)
    qi = jnp.arange(BLOCK)[:, None]
    kc = jnp.arange(3 * BLOCK)[None, :]
    rel = jnp.abs(qi + BLOCK - kc) <= WINDOW
    kpos = jnp.arange(nb)[:, None] * BLOCK - BLOCK + jnp.arange(3 * BLOCK)[None, :]
    valid = (kpos >= 0) & (kpos < S)
    mask = rel[None, :, :] & valid[:, None, :]
    s = jnp.where(mask[None, :, None, None, :, :], s, NEG_INF)
    sk = sink.astype(jnp.float32).reshape(N_KV_HEADS, Q_PER_KV)[None, None, :, :, None]
    m = jnp.maximum(jnp.max(s, axis=-1), sk)
    e = jnp.exp(s - m[..., None])
    p = e / (jnp.sum(e, axis=-1) + jnp.exp(sk - m))[..., None]
    o = jnp.einsum('bnhgqk,bnkhd->bnqhgd', p.astype(v.dtype), vb)
    return o.reshape(B, S, ATT_WIDTH)


def chunked_spatial_gating(u, z, w_s, b_s, ln_g, ln_b):
    B, S = u.shape[0], u.shape[1]
    nc = S // GM_CHUNK
    zn = layer_norm(z, ln_g, ln_b).reshape(B, nc, GM_CHUNK, GM_GROUPS, GM_GROUP_DIM)
    mixed = jnp.einsum('gpq,bnqgc->bnpgc', w_s, zn) + b_s.T[:, :, None]
    return u * mixed.reshape(B, S, GM_WIDTH)


def memory_cross_attention(xn, mn, w_q, w_kv, w_o):
    B, S = xn.shape[0], xn.shape[1]
    q = (xn @ w_q).reshape(B, S, MEM_HEADS, MEM_HEAD_DIM)
    kv = (mn @ w_kv).reshape(B, MEM_TOKENS, 2, MEM_HEADS, MEM_HEAD_DIM)
    k, v = kv[:, :, 0], kv[:, :, 1]
    s = jnp.einsum('bqhd,bkhd->bhqk', q, k).astype(jnp.float32) * (MEM_HEAD_DIM ** -0.5)
    p = jax.nn.softmax(s, axis=-1)
    o = jnp.einsum('bhqk,bkhd->bqhd', p.astype(v.dtype), v)
    return o.reshape(B, S, MEM_WIDTH) @ w_o


def peer(xn, w_q, sub_k1, sub_k2, expert_u, expert_v):
    B, S, D = xn.shape
    T = B * S
    xt = xn.reshape(T, D)
    q = (xt @ w_q).reshape(T, PEER_HEADS, PEER_QDIM)
    s1 = jnp.einsum('thc,kc->thk', q[..., :PEER_HALF], sub_k1).astype(jnp.float32)
    s2 = jnp.einsum('thc,kc->thk', q[..., PEER_HALF:], sub_k2).astype(jnp.float32)
    v1, i1 = lax.top_k(s1, PEER_TOPK)
    v2, i2 = lax.top_k(s2, PEER_TOPK)
    cand = (v1[..., :, None] + v2[..., None, :]).reshape(T, PEER_HEADS, PEER_TOPK * PEER_TOPK)
    cidx = (i1[..., :, None] * N_KEYS + i2[..., None, :]).reshape(T, PEER_HEADS, PEER_TOPK * PEER_TOPK)
    top_s, sel = lax.top_k(cand, PEER_TOPK)
    eidx = jnp.take_along_axis(cidx, sel, axis=-1)
    gate = jax.nn.softmax(top_s, axis=-1)
    nch = T // PEER_TOKEN_CHUNK
    HK = PEER_HEADS * PEER_TOPK

    def expert_chunk(args):
        xc, ec, gc = args
        ue = jnp.take(expert_u, ec, axis=0)
        h = jax.nn.gelu(jnp.einsum('cd,ced->ce', xc, ue).astype(jnp.float32))
        ve = jnp.take(expert_v, ec, axis=0)
        return jnp.einsum('ce,ced->cd', (gc * h).astype(xc.dtype), ve)

    out = lax.map(expert_chunk, (xt.reshape(nch, PEER_TOKEN_CHUNK, D),
                                 eidx.reshape(nch, PEER_TOKEN_CHUNK, HK),
                                 gate.reshape(nch, PEER_TOKEN_CHUNK, HK)))
    return out.reshape(B, S, D)


def trunk(x, mem, mix_norm, w_in, attn_sink, gm_ln_g, gm_ln_b, gm_w_s, gm_b_s,
          w_att_proj, w_gm_proj, w_out, cross_norm, mem_norm, w_q_mem, w_kv_mem, w_o_mem,
          peer_norm, w_q_peer, peer_k1, peer_k2, expert_u, expert_v, final_norm):
    B, S = x.shape[0], x.shape[1]
    pos = jnp.arange(S, dtype=jnp.float32)
    for l in range(DEPTH):
        h = rms_norm(x, mix_norm[l])
        proj = h @ w_in[l]
        q, k, v, gu, gz, ga, gb = jnp.split(proj, IN_OFFSETS, axis=-1)
        q = rotary(q.reshape(B, S, N_Q_HEADS, HEAD_DIM), pos)
        k = rotary(k.reshape(B, S, N_KV_HEADS, HEAD_DIM), pos)
        v = v.reshape(B, S, N_KV_HEADS, HEAD_DIM)
        att = banded_sink_attention(q, k, v, attn_sink[l])
        gm = chunked_spatial_gating(jax.nn.gelu(gu), jax.nn.gelu(gz), gm_w_s[l], gm_b_s[l],
                                    gm_ln_g[l], gm_ln_b[l])
        merged = jax.nn.sigmoid(ga) * (att @ w_att_proj[l]) + jax.nn.sigmoid(gb) * (gm @ w_gm_proj[l])
        x = x + merged @ w_out[l]
        h = rms_norm(x, cross_norm[l])
        mn = rms_norm(mem, mem_norm[l])
        x = x + memory_cross_attention(h, mn, w_q_mem[l], w_kv_mem[l], w_o_mem[l])
        h = rms_norm(x, peer_norm[l])
        x = x + peer(h, w_q_peer[l], peer_k1[l], peer_k2[l], expert_u[l], expert_v[l])
    return rms_norm(x, final_norm)


def setup_inputs(seed: int = 0) -> dict:
    key = jax.random.key(seed)
    ks = jax.random.split(key, 32)
    f32 = jnp.float32

    def nrm(k, shape, scale):
        return jax.random.normal(k, shape, f32) * scale

    def gain(k, shape):
        return 1.0 + 0.02 * jax.random.normal(k, shape, f32)

    return {
        "x_prompt": nrm(ks[0], (BATCH, SEQ, D_MODEL), 1.0),
        "x_sample": nrm(ks[1], (DEC_BATCH, DEC_SEQ, D_MODEL), 1.0),
        "mem_prompt": nrm(ks[2], (BATCH, MEM_TOKENS, D_MODEL), 1.0),
        "mem_sample": nrm(ks[3], (DEC_BATCH, MEM_TOKENS, D_MODEL), 1.0),
        "mix_norm": gain(ks[4], (DEPTH, D_MODEL)),
        "w_in": nrm(ks[5], (DEPTH, D_MODEL, IN_WIDTH), D_MODEL ** -0.5),
        "attn_sink": nrm(ks[6], (DEPTH, N_Q_HEADS), 0.5),
        "gm_ln_g": gain(ks[7], (DEPTH, GM_WIDTH)),
        "gm_ln_b": nrm(ks[8], (DEPTH, GM_WIDTH), 0.02),
        "gm_w_s": nrm(ks[9], (DEPTH, GM_GROUPS, GM_CHUNK, GM_CHUNK), GM_CHUNK ** -0.5),
        "gm_b_s": gain(ks[10], (DEPTH, GM_GROUPS, GM_CHUNK)),
        "w_att_proj": nrm(ks[11], (DEPTH, ATT_WIDTH, D_MODEL), ATT_WIDTH ** -0.5),
        "w_gm_proj": nrm(ks[12], (DEPTH, GM_WIDTH, D_MODEL), GM_WIDTH ** -0.5),
        "w_out": nrm(ks[13], (DEPTH, D_MODEL, D_MODEL), D_MODEL ** -0.5),
        "cross_norm": gain(ks[14], (DEPTH, D_MODEL)),
        "mem_norm": gain(ks[15], (DEPTH, D_MODEL)),
        "w_q_mem": nrm(ks[16], (DEPTH, D_MODEL, MEM_WIDTH), D_MODEL ** -0.5),
        "w_kv_mem": nrm(ks[17], (DEPTH, D_MODEL, 2 * MEM_WIDTH), D_MODEL ** -0.5),
        "w_o_mem": nrm(ks[18], (DEPTH, MEM_WIDTH, D_MODEL), MEM_WIDTH ** -0.5),
        "peer_norm": gain(ks[19], (DEPTH, D_MODEL)),
        "w_q_peer": nrm(ks[20], (DEPTH, D_MODEL, PEER_HEADS * PEER_QDIM), D_MODEL ** -0.5),
        "peer_k1": nrm(ks[21], (DEPTH, N_KEYS, PEER_HALF), PEER_HALF ** -0.5),
        "peer_k2": nrm(ks[22], (DEPTH, N_KEYS, PEER_HALF), PEER_HALF ** -0.5),
        "expert_u": nrm(ks[23], (DEPTH, N_EXPERTS, D_MODEL), D_MODEL ** -0.5),
        "expert_v": nrm(ks[24], (DEPTH, N_EXPERTS, D_MODEL), PEER_HEADS ** -0.5),
        "final_norm": gain(ks[25], (D_MODEL,)),
    }


def reference(x_prompt, x_sample, mem_prompt, mem_sample, mix_norm, w_in, attn_sink, gm_ln_g,
              gm_ln_b, gm_w_s, gm_b_s, w_att_proj, w_gm_proj, w_out, cross_norm, mem_norm,
              w_q_mem, w_kv_mem, w_o_mem, peer_norm, w_q_peer, peer_k1, peer_k2, expert_u,
              expert_v, final_norm):
    y_prompt = trunk(x_prompt, mem_prompt, mix_norm, w_in, attn_sink, gm_ln_g, gm_ln_b, gm_w_s,
                     gm_b_s, w_att_proj, w_gm_proj, w_out, cross_norm, mem_norm, w_q_mem,
                     w_kv_mem, w_o_mem, peer_norm, w_q_peer, peer_k1, peer_k2, expert_u,
                     expert_v, final_norm)
    y_sample = trunk(x_sample, mem_sample, mix_norm, w_in, attn_sink, gm_ln_g, gm_ln_b, gm_w_s,
                     gm_b_s, w_att_proj, w_gm_proj, w_out, cross_norm, mem_norm, w_q_mem,
                     w_kv_mem, w_o_mem, peer_norm, w_q_peer, peer_k1, peer_k2, expert_u,
                     expert_v, final_norm)
    return (y_prompt, y_sample)
```

```python
import functools

import numpy as np
import jax
import jax.numpy as jnp
from jax import lax
from jax.experimental import pallas as pl
from jax.experimental.pallas import tpu as pltpu

F32 = jnp.float32
BF16 = jnp.bfloat16

D_MODEL = 2048
HEAD_DIM = 128
N_Q_HEADS = 16
N_KV_HEADS = 4
Q_PER_KV = N_Q_HEADS // N_KV_HEADS
ATT_WIDTH = N_Q_HEADS * HEAD_DIM
KV_WIDTH = N_KV_HEADS * HEAD_DIM
WINDOW = 128
ROPE_THETA = 10000.0
GM_WIDTH = 2048
GM_GROUPS = 16
GM_CHUNK = 128
MEM_TOKENS = 256
MEM_HEADS = 4
MEM_HEAD_DIM = 128
MEM_WIDTH = MEM_HEADS * MEM_HEAD_DIM
PEER_HEADS = 8
PEER_QDIM = 256
PEER_HALF = PEER_QDIM // 2
N_KEYS = 128
N_EXPERTS = N_KEYS * N_KEYS
PEER_TOPK = 16
NORM_EPS = 1e-6
NEG_INF = -1e30

COL_Q, COL_GU, COL_GZ, COL_GA, COL_GB = 0, 1, 2, 3, 4
COL_K = (ATT_WIDTH + 4 * D_MODEL) // KV_WIDTH
COL_V = COL_K + 1
IN_WIDTH = ATT_WIDTH + 2 * KV_WIDTH + 2 * GM_WIDTH + 2 * D_MODEL

ATT_BLOCK = 512
VMEM_LIMIT = 56 * 1024 * 1024


def _cparams(*sem):
    return pltpu.CompilerParams(dimension_semantics=sem, vmem_limit_bytes=VMEM_LIMIT)


def _rms(x, g):
    return x * lax.rsqrt(jnp.mean(x * x, axis=-1, keepdims=True) + NORM_EPS) * g


def _dot_nt(a, b):
    return lax.dot_general(a, b, (((1,), (1,)), ((), ())), preferred_element_type=F32)


def _norm_matmul_kernel(x_ref, g_ref, w_ref, o_ref, xn_ref):
    @pl.when(pl.program_id(1) == 0)
    def _():
        xn_ref[...] = _rms(x_ref[...], g_ref[...]).astype(BF16)

    o_ref[...] = jnp.dot(xn_ref[...], w_ref[...], preferred_element_type=F32).astype(o_ref.dtype)


def norm_matmul(x, g, w, *, bm, bn):
    t, d = x.shape
    n = w.shape[1]
    assert t % bm == 0 and n % bn == 0
    return pl.pallas_call(
        _norm_matmul_kernel,
        grid=(t // bm, n // bn),
        in_specs=[
            pl.BlockSpec((bm, d), lambda i, j: (i, 0)),
            pl.BlockSpec((1, d), lambda i, j: (0, 0)),
            pl.BlockSpec((d, bn), lambda i, j: (0, j)),
        ],
        out_specs=pl.BlockSpec((bm, bn), lambda i, j: (i, j)),
        out_shape=jax.ShapeDtypeStruct((t, n), BF16),
        scratch_shapes=[pltpu.VMEM((bm, d), BF16)],
        compiler_params=_cparams("parallel", "arbitrary"),
        name="norm_matmul",
    )(x, g.reshape(1, d), w)


def _attn_kernel(meta_ref, q_ref, kc_ref, kp_ref, kn_ref, vc_ref, vp_ref, vn_ref,
                 csc_ref, csp_ref, csn_ref, sink_ref, o_ref, qr_scr, kr_scr, v_scr):
    n = pl.program_id(0)
    nb = ATT_BLOCK // WINDOW
    band = 3 * WINDOW
    k_lo = jnp.where(meta_ref[1, n] == 1, WINDOW, 0)
    k_hi = jnp.where(meta_ref[2, n] == 1, 2 * WINDOW, band)

    def rope(x, cs):
        return x * cs[:, :HEAD_DIM] + pltpu.roll(x, HEAD_DIM // 2, 1) * cs[:, HEAD_DIM:]

    cs_c = csc_ref[...]
    for hq in range(N_Q_HEADS):
        sl = slice(hq * HEAD_DIM, (hq + 1) * HEAD_DIM)
        qr_scr[:, sl] = rope(q_ref[:, sl].astype(F32), cs_c).astype(BF16)
    cs_p = csp_ref[...]
    cs_n = csn_ref[...]
    for h in range(N_KV_HEADS):
        sl = slice(h * HEAD_DIM, (h + 1) * HEAD_DIM)
        kr_scr[0:WINDOW, sl] = rope(kp_ref[:, sl].astype(F32), cs_p).astype(BF16)
        kr_scr[WINDOW:WINDOW + ATT_BLOCK, sl] = rope(kc_ref[:, sl].astype(F32), cs_c).astype(BF16)
        kr_scr[WINDOW + ATT_BLOCK:, sl] = rope(kn_ref[:, sl].astype(F32), cs_n).astype(BF16)
    v_scr[0:WINDOW, :] = vp_ref[...]
    v_scr[WINDOW:WINDOW + ATT_BLOCK, :] = vc_ref[...]
    v_scr[WINDOW + ATT_BLOCK:, :] = vn_ref[...]

    rows = Q_PER_KV * WINDOW
    qi = lax.broadcasted_iota(jnp.int32, (rows, band), 0) % WINDOW
    kc = lax.broadcasted_iota(jnp.int32, (rows, band), 1)
    rel = jnp.abs(qi + WINDOW - kc) <= WINDOW
    scale = HEAD_DIM ** -0.5
    for r in range(nb):
        mask = rel
        if r == 0:
            mask = jnp.logical_and(mask, kc >= k_lo)
        if r == nb - 1:
            mask = jnp.logical_and(mask, kc < k_hi)
        r0 = r * WINDOW
        for h in range(N_KV_HEADS):
            qs = jnp.concatenate(
                [qr_scr[r0:r0 + WINDOW, (Q_PER_KV * h + g) * HEAD_DIM:(Q_PER_KV * h + g + 1) * HEAD_DIM]
                 for g in range(Q_PER_KV)], axis=0)
            kb = kr_scr[r0:r0 + band, h * HEAD_DIM:(h + 1) * HEAD_DIM]
            s = _dot_nt(qs, kb) * scale
            s = jnp.where(mask, s, NEG_INF)
            sk = sink_ref[h]
            m = jnp.maximum(jnp.max(s, axis=-1, keepdims=True), sk)
            e = jnp.exp(s - m)
            den = jnp.sum(e, axis=-1, keepdims=True) + jnp.exp(sk - m)
            p = (e / den).astype(BF16)
            o = jnp.dot(p, v_scr[r0:r0 + band, h * HEAD_DIM:(h + 1) * HEAD_DIM],
                        preferred_element_type=F32)
            for g in range(Q_PER_KV):
                c0 = (Q_PER_KV * h + g) * HEAD_DIM
                o_ref[r0:r0 + WINDOW, c0:c0 + HEAD_DIM] = o[g * WINDOW:(g + 1) * WINDOW].astype(BF16)


def banded_attention(proj, meta, cs_table, sink_col):
    t = proj.shape[0]
    nblk = t // ATT_BLOCK
    sub = ATT_BLOCK // WINDOW
    n_small = t // WINDOW
    n_pos_small = cs_table.shape[0] // WINDOW

    def cur(n, m):
        return (n, 0)

    grid_spec = pltpu.PrefetchScalarGridSpec(
        num_scalar_prefetch=1,
        grid=(nblk,),
        in_specs=[
            pl.BlockSpec((ATT_BLOCK, ATT_WIDTH), lambda n, m: (n, COL_Q)),
            pl.BlockSpec((ATT_BLOCK, KV_WIDTH), lambda n, m: (n, COL_K)),
            pl.BlockSpec((WINDOW, KV_WIDTH), lambda n, m: (jnp.maximum(n * sub - 1, 0), COL_K)),
            pl.BlockSpec((WINDOW, KV_WIDTH), lambda n, m: (jnp.minimum(n * sub + sub, n_small - 1), COL_K)),
            pl.BlockSpec((ATT_BLOCK, KV_WIDTH), lambda n, m: (n, COL_V)),
            pl.BlockSpec((WINDOW, KV_WIDTH), lambda n, m: (jnp.maximum(n * sub - 1, 0), COL_V)),
            pl.BlockSpec((WINDOW, KV_WIDTH), lambda n, m: (jnp.minimum(n * sub + sub, n_small - 1), COL_V)),
            pl.BlockSpec((ATT_BLOCK, 2 * HEAD_DIM), lambda n, m: (m[0, n], 0)),
            pl.BlockSpec((WINDOW, 2 * HEAD_DIM), lambda n, m: (jnp.maximum(m[0, n] * sub - 1, 0), 0)),
            pl.BlockSpec((WINDOW, 2 * HEAD_DIM),
                         lambda n, m: (jnp.minimum(m[0, n] * sub + sub, n_pos_small - 1), 0)),
            pl.BlockSpec((N_KV_HEADS, Q_PER_KV * WINDOW, 1), lambda n, m: (0, 0, 0)),
        ],
        out_specs=pl.BlockSpec((ATT_BLOCK, ATT_WIDTH), cur),
        scratch_shapes=[
            pltpu.VMEM((ATT_BLOCK, ATT_WIDTH), BF16),
            pltpu.VMEM((ATT_BLOCK + 2 * WINDOW, KV_WIDTH), BF16),
            pltpu.VMEM((ATT_BLOCK + 2 * WINDOW, KV_WIDTH), BF16),
        ],
    )
    return pl.pallas_call(
        _attn_kernel,
        grid_spec=grid_spec,
        out_shape=jax.ShapeDtypeStruct((t, ATT_WIDTH), BF16),
        compiler_params=_cparams("parallel"),
        name="banded_attention",
    )(meta, proj, proj, proj, proj, proj, proj, proj, cs_table, cs_table, cs_table, sink_col)


def _gmlp_kernel(gu_ref, gz_ref, lng_ref, lnb_ref, ws_ref, bs_ref, o_ref, zn_scr):
    z = jax.nn.gelu(gz_ref[...].astype(F32))
    mu = jnp.mean(z, axis=-1, keepdims=True)
    zc = z - mu
    var = jnp.mean(zc * zc, axis=-1, keepdims=True)
    zn_scr[...] = (zc * lax.rsqrt(var + NORM_EPS) * lng_ref[...] + lnb_ref[...]).astype(BF16)
    gd = GM_WIDTH // GM_GROUPS
    for c in range(o_ref.shape[0] // GM_CHUNK):
        rs = slice(c * GM_CHUNK, (c + 1) * GM_CHUNK)
        for g in range(GM_GROUPS):
            cs = slice(g * gd, (g + 1) * gd)
            mixed = jnp.dot(ws_ref[g], zn_scr[rs, cs], preferred_element_type=F32) + bs_ref[g]
            u = jax.nn.gelu(gu_ref[rs, cs].astype(F32))
            o_ref[rs, cs] = (u * mixed).astype(BF16)


def spatial_gating(proj, ln_g, ln_b, w_s, b_s_b, *, bm):
    t = proj.shape[0]
    return pl.pallas_call(
        _gmlp_kernel,
        grid=(t // bm,),
        in_specs=[
            pl.BlockSpec((bm, GM_WIDTH), lambda i: (i, COL_GU)),
            pl.BlockSpec((bm, GM_WIDTH), lambda i: (i, COL_GZ)),
            pl.BlockSpec((1, GM_WIDTH), lambda i: (0, 0)),
            pl.BlockSpec((1, GM_WIDTH), lambda i: (0, 0)),
            pl.BlockSpec((GM_GROUPS, GM_CHUNK, GM_CHUNK), lambda i: (0, 0, 0)),
            pl.BlockSpec((GM_GROUPS, GM_CHUNK, GM_WIDTH // GM_GROUPS), lambda i: (0, 0, 0)),
        ],
        out_specs=pl.BlockSpec((bm, GM_WIDTH), lambda i: (i, 0)),
        out_shape=jax.ShapeDtypeStruct((t, GM_WIDTH), BF16),
        scratch_shapes=[pltpu.VMEM((bm, GM_WIDTH), BF16)],
        compiler_params=_cparams("parallel"),
        name="spatial_gating",
    )(proj, proj, ln_g.reshape(1, -1), ln_b.reshape(1, -1), w_s, b_s_b)


def _sigmoid(x):
    return 1.0 / (1.0 + jnp.exp(-x))


def _merge_kernel(att_ref, gm_ref, ga_ref, gb_ref, x_ref, wa_ref, wg_ref, wo_ref, y_ref):
    a = jnp.dot(att_ref[...], wa_ref[...], preferred_element_type=F32)
    b = jnp.dot(gm_ref[...], wg_ref[...], preferred_element_type=F32)
    merged = _sigmoid(ga_ref[...].astype(F32)) * a + _sigmoid(gb_ref[...].astype(F32)) * b
    y_ref[...] = x_ref[...] + jnp.dot(merged.astype(BF16), wo_ref[...], preferred_element_type=F32)


def merge_project(att, gm, proj, x, wa, wg, wo, *, bm):
    t, d = x.shape
    const = dict(pipeline_mode=pl.Buffered(1))
    return pl.pallas_call(
        _merge_kernel,
        grid=(t // bm,),
        in_specs=[
            pl.BlockSpec((bm, ATT_WIDTH), lambda i: (i, 0)),
            pl.BlockSpec((bm, GM_WIDTH), lambda i: (i, 0)),
            pl.BlockSpec((bm, d), lambda i: (i, COL_GA)),
            pl.BlockSpec((bm, d), lambda i: (i, COL_GB)),
            pl.BlockSpec((bm, d), lambda i: (i, 0)),
            pl.BlockSpec((ATT_WIDTH, d), lambda i: (0, 0), **const),
            pl.BlockSpec((GM_WIDTH, d), lambda i: (0, 0), **const),
            pl.BlockSpec((d, d), lambda i: (0, 0), **const),
        ],
        out_specs=pl.BlockSpec((bm, d), lambda i: (i, 0)),
        out_shape=jax.ShapeDtypeStruct((t, d), F32),
        compiler_params=_cparams("parallel"),
        name="merge_project",
    )(att, gm, proj, proj, x, wa, wg, wo)


def _cross_kernel(meta_ref, x_ref, g_ref, wq_ref, kv_ref, wo_ref, y_ref):
    x = x_ref[...]
    xn = _rms(x, g_ref[...]).astype(BF16)
    q = jnp.dot(xn, wq_ref[...], preferred_element_type=F32).astype(BF16)
    scale = MEM_HEAD_DIM ** -0.5
    outs = []
    for h in range(MEM_HEADS):
        sl = slice(h * MEM_HEAD_DIM, (h + 1) * MEM_HEAD_DIM)
        kh = kv_ref[0, :, sl]
        vh = kv_ref[0, :, MEM_WIDTH + h * MEM_HEAD_DIM:MEM_WIDTH + (h + 1) * MEM_HEAD_DIM]
        s = _dot_nt(q[:, sl], kh) * scale
        m = jnp.max(s, axis=-1, keepdims=True)
        e = jnp.exp(s - m)
        p = (e / jnp.sum(e, axis=-1, keepdims=True)).astype(BF16)
        outs.append(jnp.dot(p, vh, preferred_element_type=F32).astype(BF16))
    o = jnp.concatenate(outs, axis=-1)
    y_ref[...] = x + jnp.dot(o, wo_ref[...], preferred_element_type=F32)


def cross_attention(x, meta, g, wq, kv, wo, *, bm):
    t, d = x.shape
    grid_spec = pltpu.PrefetchScalarGridSpec(
        num_scalar_prefetch=1,
        grid=(t // bm,),
        in_specs=[
            pl.BlockSpec((bm, d), lambda i, m: (i, 0)),
            pl.BlockSpec((1, d), lambda i, m: (0, 0)),
            pl.BlockSpec((d, MEM_WIDTH), lambda i, m: (0, 0)),
            pl.BlockSpec((1, MEM_TOKENS, 2 * MEM_WIDTH), lambda i, m: (m[3, i], 0, 0)),
            pl.BlockSpec((MEM_WIDTH, d), lambda i, m: (0, 0)),
        ],
        out_specs=pl.BlockSpec((bm, d), lambda i, m: (i, 0)),
    )
    return pl.pallas_call(
        _cross_kernel,
        grid_spec=grid_spec,
        out_shape=jax.ShapeDtypeStruct((t, d), F32),
        compiler_params=_cparams("parallel"),
        name="cross_attention",
    )(meta, x, g.reshape(1, d), wq, kv, wo)


N_TOP = PEER_TOPK + 1
CAND = [(a, b) for a in range(N_TOP) for b in range(N_TOP // (a + 1))]
CAND_ROWS = -(-len(CAND) // 8) * 8


def _top_values(s, n):
    vals = []
    for _ in range(n):
        m = jnp.max(s, axis=0, keepdims=True)
        vals.append(m)
        s = jnp.where(s == m, NEG_INF, s)
    return vals


def _router_kernel(x_ref, g_ref, wqt_ref, k1_ref, k2_ref, xn_ref, s2_ref, bz_ref, a_ref, d1_ref,
                   qt_scr, c_scr):
    xn = _rms(x_ref[...], g_ref[...]).astype(BF16)
    xn_ref[...] = xn
    qt_scr[...] = _dot_nt(wqt_ref[...], xn).astype(BF16)
    bm = xn.shape[0]
    c_scr[...] = jnp.full(c_scr.shape, NEG_INF, F32)

    def head(h, carry):
        off = pl.multiple_of(h * PEER_QDIM, PEER_QDIM)
        s1 = jnp.dot(k1_ref[...], qt_scr[pl.ds(off, PEER_HALF), :], preferred_element_type=F32)
        s2 = jnp.dot(k2_ref[...], qt_scr[pl.ds(off + PEER_HALF, PEER_HALF), :],
                     preferred_element_type=F32)
        v1 = _top_values(s1, N_TOP)
        v2 = _top_values(s2, N_TOP)
        for r, (a, b) in enumerate(CAND):
            c_scr[r:r + 1, :] = v1[a] + v2[b]
        c0 = c_scr[...]
        c = c0
        cum = jnp.zeros((1, bm), F32)
        t_in = jnp.full((1, bm), NEG_INF, F32)
        t_out = jnp.full((1, bm), NEG_INF, F32)
        for _ in range(N_TOP):
            m = jnp.max(c, axis=0, keepdims=True)
            hit = c == m
            cnt = jnp.sum(jnp.where(hit, 1.0, 0.0), axis=0, keepdims=True)
            t_in = jnp.where(cum < PEER_TOPK, m, t_in)
            t_out = jnp.where(cum >= PEER_TOPK, jnp.maximum(t_out, m), t_out)
            c = jnp.where(hit, NEG_INF, c)
            cum = cum + cnt
        thr = 0.5 * (t_in + t_out)
        top = v1[0] + v2[0]
        z = jnp.sum(jnp.where(c0 >= thr, jnp.exp(c0 - top), 0.0), axis=0, keepdims=True)
        s2_ref[h] = s2
        bz_ref[h] = jnp.exp(s2 - v2[0]) / z
        a_ref[h] = jnp.exp(s1 - v1[0])
        d1_ref[h] = thr - s1
        return carry

    lax.fori_loop(0, PEER_HEADS, head, 0)


def peer_router(x, g, wqt, k1, k2, *, bm):
    t, d = x.shape
    head_spec = pl.BlockSpec((PEER_HEADS, N_KEYS, bm), lambda i: (0, 0, i))
    head_shape = jax.ShapeDtypeStruct((PEER_HEADS, N_KEYS, t), F32)
    return pl.pallas_call(
        _router_kernel,
        grid=(t // bm,),
        in_specs=[
            pl.BlockSpec((bm, d), lambda i: (i, 0)),
            pl.BlockSpec((1, d), lambda i: (0, 0)),
            pl.BlockSpec((PEER_HEADS * PEER_QDIM, d), lambda i: (0, 0)),
            pl.BlockSpec((N_KEYS, PEER_HALF), lambda i: (0, 0)),
            pl.BlockSpec((N_KEYS, PEER_HALF), lambda i: (0, 0)),
        ],
        out_specs=[pl.BlockSpec((bm, d), lambda i: (i, 0)), head_spec, head_spec, head_spec, head_spec],
        out_shape=[jax.ShapeDtypeStruct((t, d), BF16), head_shape, head_shape, head_shape, head_shape],
        scratch_shapes=[pltpu.VMEM((PEER_HEADS * PEER_QDIM, bm), BF16), pltpu.VMEM((CAND_ROWS, bm), F32)],
        compiler_params=_cparams("parallel"),
        name="peer_router",
    )(x, g.reshape(1, d), wqt, k1, k2)


EXPERT_TILE = 1024
KEYS_PER_TILE = EXPERT_TILE // N_KEYS


def _expert_kernel(xn_ref, u_ref, vt_ref, s2_ref, bz_ref, a_ref, d1_ref, x_ref, y_ref,
                   h_scr, p_scr, acc_scr):
    e = pl.program_id(1)
    bm = xn_ref.shape[0]

    @pl.when(e == 0)
    def _():
        acc_scr[...] = jnp.zeros(acc_scr.shape, F32)

    h_scr[...] = _dot_nt(u_ref[...], xn_ref[...])

    for il in range(KEYS_PER_TILE):
        rows = slice(il * N_KEYS, (il + 1) * N_KEYS)
        for tg in range(bm // 128):
            ts = slice(tg * 128, (tg + 1) * 128)
            w = jnp.zeros((N_KEYS, 128), F32)
            for h in range(PEER_HEADS):
                d1 = d1_ref[h, il:il + 1, ts]
                a = a_ref[h, il:il + 1, ts]
                w = w + jnp.where(s2_ref[h, :, ts] >= d1, bz_ref[h, :, ts], 0.0) * a
            p_scr[rows, ts] = (jax.nn.gelu(h_scr[rows, ts]) * w).astype(BF16)
    acc_scr[...] += jnp.dot(vt_ref[...], p_scr[...], preferred_element_type=F32)

    @pl.when(e == pl.num_programs(1) - 1)
    def _():
        step = 256
        for c in range(0, y_ref.shape[1], step):
            y_ref[:, c:c + step] = x_ref[:, c:c + step] + acc_scr[c:c + step, :].T


def peer_experts(xn, u, vt, s2, bz, a, d1, x, *, bm):
    t, d = x.shape
    n_e = u.shape[0]
    head_full = pl.BlockSpec((PEER_HEADS, N_KEYS, bm), lambda i, e: (0, 0, i))
    head_rows = pl.BlockSpec((PEER_HEADS, KEYS_PER_TILE, bm), lambda i, e: (0, e, i))
    return pl.pallas_call(
        _expert_kernel,
        grid=(t // bm, n_e // EXPERT_TILE),
        in_specs=[
            pl.BlockSpec((bm, d), lambda i, e: (i, 0)),
            pl.BlockSpec((EXPERT_TILE, d), lambda i, e: (e, 0)),
            pl.BlockSpec((d, EXPERT_TILE), lambda i, e: (0, e)),
            head_full, head_full, head_rows, head_rows,
            pl.BlockSpec((bm, d), lambda i, e: (i, 0)),
        ],
        out_specs=pl.BlockSpec((bm, d), lambda i, e: (i, 0)),
        out_shape=jax.ShapeDtypeStruct((t, d), F32),
        scratch_shapes=[
            pltpu.VMEM((EXPERT_TILE, bm), F32),
            pltpu.VMEM((EXPERT_TILE, bm), BF16),
            pltpu.VMEM((d, bm), F32),
        ],
        compiler_params=_cparams("parallel", "arbitrary"),
        name="peer_experts",
    )(xn, u, vt, s2, bz, a, d1, x)


def _final_norm_kernel(x_ref, g_ref, y_ref):
    y_ref[...] = _rms(x_ref[...], g_ref[...])


def final_norm_slice(x, g, start, rows, *, bm):
    d = x.shape[1]
    off = start // bm
    return pl.pallas_call(
        _final_norm_kernel,
        grid=(rows // bm,),
        in_specs=[pl.BlockSpec((bm, d), lambda i: (i + off, 0)), pl.BlockSpec((1, d), lambda i: (0, 0))],
        out_specs=pl.BlockSpec((bm, d), lambda i: (i, 0)),
        out_shape=jax.ShapeDtypeStruct((rows, d), F32),
        compiler_params=_cparams("parallel"),
        name="final_norm",
    )(x, g.reshape(1, d))


def _block_meta(groups, blk):
    pos, first, last, batch = [], [], [], []
    b0 = 0
    for (b, s) in groups:
        per = s // blk
        for bi in range(b):
            for j in range(per):
                pos.append(j)
                first.append(int(j == 0))
                last.append(int(j == per - 1))
                batch.append(b0 + bi)
        b0 += b
    return jnp.asarray(np.array([pos, first, last, batch], dtype=np.int32))


def _rope_table(max_pos):
    half = HEAD_DIM // 2
    inv = ROPE_THETA ** (-jnp.arange(half, dtype=F32) / half)
    ang = jnp.arange(max_pos, dtype=F32)[:, None] * inv[None, :]
    cos, sin = jnp.cos(ang), jnp.sin(ang)
    return jnp.concatenate([cos, cos, -sin, sin], axis=-1)


def _pick(t, pref):
    while t % pref:
        pref //= 2
    return pref


def _trunk(xs, mems, mix_norm, w_in, attn_sink, gm_ln_g, gm_ln_b, gm_w_s, gm_b_s, w_att_proj, w_gm_proj,
           w_out, cross_norm, mem_norm, w_q_mem, w_kv_mem, w_o_mem, peer_norm, w_q_peer, peer_k1, peer_k2,
           expert_u, expert_v, final_norm):
    groups = [(x.shape[0], x.shape[1]) for x in xs]
    depth = w_in.shape[0]
    for (_, s) in groups:
        assert s % ATT_BLOCK == 0
    x = jnp.concatenate([x.reshape(-1, D_MODEL) for x in xs], axis=0)
    mem = jnp.concatenate([m.reshape(-1, D_MODEL) for m in mems], axis=0)
    t = x.shape[0]
    n_mem_rows = mem.shape[0]
    meta = _block_meta(groups, ATT_BLOCK)
    cs_table = _rope_table(max(s for _, s in groups))

    bm_big = _pick(t, 1024)
    bm_mid = _pick(t, 512)
    bm_small = _pick(t, 256)

    for l in range(depth):
        w_in_l = w_in[l]
        q_w, k_w, v_w, gu_w, gz_w, ga_w, gb_w = jnp.split(
            w_in_l, np.cumsum([ATT_WIDTH, KV_WIDTH, KV_WIDTH, GM_WIDTH, GM_WIDTH, D_MODEL])[:].tolist(), axis=1)
        w_in_p = jnp.concatenate([q_w, gu_w, gz_w, ga_w, gb_w, k_w, v_w], axis=1).astype(BF16)
        sink_col = jnp.repeat(attn_sink[l].astype(F32).reshape(N_KV_HEADS, Q_PER_KV), WINDOW, axis=1)
        sink_col = sink_col.reshape(N_KV_HEADS, Q_PER_KV * WINDOW, 1)
        b_s_b = jnp.broadcast_to(gm_b_s[l].astype(F32)[:, :, None],
                                 (GM_GROUPS, GM_CHUNK, GM_WIDTH // GM_GROUPS))

        proj = norm_matmul(x, mix_norm[l], w_in_p, bm=bm_big, bn=1024)
        att = banded_attention(proj, meta, cs_table, sink_col)
        gm = spatial_gating(proj, gm_ln_g[l], gm_ln_b[l], gm_w_s[l].astype(BF16), b_s_b, bm=ATT_BLOCK)
        x = merge_project(att, gm, proj, x, w_att_proj[l].astype(BF16), w_gm_proj[l].astype(BF16),
                          w_out[l].astype(BF16), bm=bm_small)

        kv = norm_matmul(mem, mem_norm[l], w_kv_mem[l].astype(BF16), bm=_pick(n_mem_rows, 512), bn=1024)
        kv = kv.reshape(-1, MEM_TOKENS, 2 * MEM_WIDTH)
        x = cross_attention(x, meta, cross_norm[l], w_q_mem[l].astype(BF16), kv, w_o_mem[l].astype(BF16),
                            bm=ATT_BLOCK)

        xn, s2, bz, a, d1 = peer_router(x, peer_norm[l], w_q_peer[l].T.astype(BF16),
                                        peer_k1[l].astype(BF16), peer_k2[l].astype(BF16), bm=bm_small)
        x = peer_experts(xn, expert_u[l].astype(BF16), expert_v[l].T.astype(BF16), s2, bz, a, d1, x,
                         bm=bm_mid)

    outs = []
    start = 0
    for (b, s), x_in in zip(groups, xs):
        rows = b * s
        y = final_norm_slice(x, final_norm, start, rows, bm=_pick(int(np.gcd(rows, start)), 1024))
        outs.append(y.reshape(b, s, D_MODEL))
        start += rows
    return tuple(outs)


def kernel(x_prompt, x_sample, mem_prompt, mem_sample, mix_norm, w_in, attn_sink, gm_ln_g, gm_ln_b, gm_w_s, gm_b_s, w_att_proj, w_gm_proj, w_out, cross_norm, mem_norm, w_q_mem, w_kv_mem, w_o_mem, peer_norm, w_q_peer, peer_k1, peer_k2, expert_u, expert_v, final_norm):
    return _trunk((x_prompt, x_sample), (mem_prompt, mem_sample), mix_norm, w_in, attn_sink, gm_ln_g,
                  gm_ln_b, gm_w_s, gm_b_s, w_att_proj, w_gm_proj, w_out, cross_norm, mem_norm, w_q_mem,
                  w_kv_mem, w_o_mem, peer_norm, w_q_peer, peer_k1, peer_k2, expert_u, expert_v, final_norm)
```

```python
import numpy as np
import jax
import jax.numpy as jnp
from jax import lax
from jax.experimental import pallas as pl
from jax.experimental.pallas import tpu as pltpu

F32 = jnp.float32
BF16 = jnp.bfloat16

D_MODEL = 2048
HEAD_DIM = 128
N_Q_HEADS = 16
N_KV_HEADS = 4
Q_PER_KV = N_Q_HEADS // N_KV_HEADS
ATT_WIDTH = N_Q_HEADS * HEAD_DIM
KV_WIDTH = N_KV_HEADS * HEAD_DIM
WINDOW = 128
ROPE_THETA = 10000.0
GM_WIDTH = 2048
GM_GROUPS = 16
GM_CHUNK = 128
MEM_TOKENS = 256
MEM_HEADS = 4
MEM_HEAD_DIM = 128
MEM_WIDTH = MEM_HEADS * MEM_HEAD_DIM
PEER_HEADS = 8
PEER_QDIM = 256
PEER_HALF = PEER_QDIM // 2
N_KEYS = 128
N_EXPERTS = N_KEYS * N_KEYS
PEER_TOPK = 16
NORM_EPS = 1e-6
NEG_INF = -1e30

COL_Q, COL_GU, COL_GZ, COL_GA, COL_GB = 0, 1, 2, 3, 4
COL_K = (ATT_WIDTH + 4 * D_MODEL) // KV_WIDTH
COL_V = COL_K + 1

LANES = 128
SUBLANES = 8
ATT_BLOCK = 512
SINK_LANES = LANES
VMEM_LIMIT = 56 * 1024 * 1024


def _cparams(*sem):
    return pltpu.CompilerParams(dimension_semantics=sem, vmem_limit_bytes=VMEM_LIMIT)


def _rms(x, g):
    return x * lax.rsqrt(jnp.mean(x * x, axis=-1, keepdims=True) + NORM_EPS) * g


def _dot_nt(a, b):
    return lax.dot_general(a, b, (((1,), (1,)), ((), ())), preferred_element_type=F32)


def _norm_matmul_kernel(x_ref, g_ref, w_ref, o_ref, xn_ref):
    @pl.when(pl.program_id(1) == 0)
    def _():
        xn_ref[...] = _rms(x_ref[...], g_ref[...]).astype(BF16)

    o_ref[...] = jnp.dot(xn_ref[...], w_ref[...], preferred_element_type=F32).astype(o_ref.dtype)


def norm_matmul(x, g, w, *, bm, bn):
    t, d = x.shape
    n = w.shape[1]
    assert t % bm == 0 and n % bn == 0
    return pl.pallas_call(
        _norm_matmul_kernel,
        grid=(t // bm, n // bn),
        in_specs=[
            pl.BlockSpec((bm, d), lambda i, j: (i, 0)),
            pl.BlockSpec((1, d), lambda i, j: (0, 0)),
            pl.BlockSpec((d, bn), lambda i, j: (0, j)),
        ],
        out_specs=pl.BlockSpec((bm, bn), lambda i, j: (i, j)),
        out_shape=jax.ShapeDtypeStruct((t, n), BF16),
        scratch_shapes=[pltpu.VMEM((bm, d), BF16)],
        compiler_params=_cparams("parallel", "arbitrary"),
        name="norm_matmul",
    )(x, g.reshape(1, d), w)


def _attn_kernel(meta_ref, q_ref, kc_ref, kp_ref, kn_ref, vc_ref, vp_ref, vn_ref,
                 csc_ref, csp_ref, csn_ref, sink_ref, o_ref, q_scr, k_scr, v_scr, o_scr):
    n = pl.program_id(0)
    nb = ATT_BLOCK // WINDOW
    band = 3 * WINDOW
    k_lo = jnp.where(meta_ref[1, n] == 1, WINDOW, 0)
    k_hi = jnp.where(meta_ref[2, n] == 1, 2 * WINDOW, band)

    def rope(x, cs):
        return x * cs[:, :HEAD_DIM] + pltpu.roll(x, HEAD_DIM // 2, 1) * cs[:, HEAD_DIM:]

    cs_c = csc_ref[...]
    for hq in range(N_Q_HEADS):
        sl = slice(hq * HEAD_DIM, (hq + 1) * HEAD_DIM)
        q_scr[hq] = rope(q_ref[:, sl].astype(F32), cs_c).astype(BF16)
    cs_p = csp_ref[...]
    cs_n = csn_ref[...]
    for h in range(N_KV_HEADS):
        sl = slice(h * HEAD_DIM, (h + 1) * HEAD_DIM)
        k_scr[h, 0:WINDOW, :] = rope(kp_ref[:, sl].astype(F32), cs_p).astype(BF16)
        k_scr[h, WINDOW:WINDOW + ATT_BLOCK, :] = rope(kc_ref[:, sl].astype(F32), cs_c).astype(BF16)
        k_scr[h, WINDOW + ATT_BLOCK:, :] = rope(kn_ref[:, sl].astype(F32), cs_n).astype(BF16)
        v_scr[h, 0:WINDOW, :] = vp_ref[:, sl]
        v_scr[h, WINDOW:WINDOW + ATT_BLOCK, :] = vc_ref[:, sl]
        v_scr[h, WINDOW + ATT_BLOCK:, :] = vn_ref[:, sl]

    qi = lax.broadcasted_iota(jnp.int32, (WINDOW, band), 0)
    kc = lax.broadcasted_iota(jnp.int32, (WINDOW, band), 1)
    rel = jnp.abs(qi + WINDOW - kc) <= WINDOW
    scale = HEAD_DIM ** -0.5

    def band_head(idx, carry):
        r = idx // N_KV_HEADS
        h = idx % N_KV_HEADS
        r0 = pl.multiple_of(r * WINDOW, WINDOW)
        lo = jnp.where(r == 0, k_lo, 0)
        hi = jnp.where(r == nb - 1, k_hi, band)
        bias = jnp.where(jnp.logical_and(rel, jnp.logical_and(kc >= lo, kc < hi)), 0.0, NEG_INF)
        kb = k_scr[h, pl.ds(r0, band), :]
        vb = v_scr[h, pl.ds(r0, band), :]
        scores = [_dot_nt(q_scr[Q_PER_KV * h + g, pl.ds(r0, WINDOW), :], kb) for g in range(Q_PER_KV)]
        for g in range(Q_PER_KV):
            hq = Q_PER_KV * h + g
            s = scores[g] * scale + bias
            sk = sink_ref[hq]
            m = jnp.maximum(jnp.max(s, axis=-1, keepdims=True), sk)
            e = jnp.concatenate([jnp.exp(s[:, j * WINDOW:(j + 1) * WINDOW] - m) for j in range(3)], axis=-1)
            den = jnp.sum(e, axis=-1, keepdims=True) + jnp.exp(sk - m)
            o = jnp.dot(e.astype(BF16), vb, preferred_element_type=F32) * (1.0 / den)
            o_scr[hq, pl.ds(r0, WINDOW), :] = o.astype(BF16)
        return carry

    lax.fori_loop(0, nb * N_KV_HEADS, band_head, 0)
    for hq in range(N_Q_HEADS):
        o_ref[:, hq * HEAD_DIM:(hq + 1) * HEAD_DIM] = o_scr[hq]


def banded_attention(proj, meta, cs_table, sink_col):
    t = proj.shape[0]
    nblk = t // ATT_BLOCK
    sub = ATT_BLOCK // WINDOW
    n_small = t // WINDOW
    n_pos_small = cs_table.shape[0] // WINDOW
    grid_spec = pltpu.PrefetchScalarGridSpec(
        num_scalar_prefetch=1,
        grid=(nblk,),
        in_specs=[
            pl.BlockSpec((ATT_BLOCK, ATT_WIDTH), lambda n, m: (n, COL_Q)),
            pl.BlockSpec((ATT_BLOCK, KV_WIDTH), lambda n, m: (n, COL_K)),
            pl.BlockSpec((WINDOW, KV_WIDTH), lambda n, m: (jnp.maximum(n * sub - 1, 0), COL_K)),
            pl.BlockSpec((WINDOW, KV_WIDTH), lambda n, m: (jnp.minimum(n * sub + sub, n_small - 1), COL_K)),
            pl.BlockSpec((ATT_BLOCK, KV_WIDTH), lambda n, m: (n, COL_V)),
            pl.BlockSpec((WINDOW, KV_WIDTH), lambda n, m: (jnp.maximum(n * sub - 1, 0), COL_V)),
            pl.BlockSpec((WINDOW, KV_WIDTH), lambda n, m: (jnp.minimum(n * sub + sub, n_small - 1), COL_V)),
            pl.BlockSpec((ATT_BLOCK, 2 * HEAD_DIM), lambda n, m: (m[0, n], 0)),
            pl.BlockSpec((WINDOW, 2 * HEAD_DIM), lambda n, m: (jnp.maximum(m[0, n] * sub - 1, 0), 0)),
            pl.BlockSpec((WINDOW, 2 * HEAD_DIM),
                         lambda n, m: (jnp.minimum(m[0, n] * sub + sub, n_pos_small - 1), 0)),
            pl.BlockSpec((N_Q_HEADS, WINDOW, SINK_LANES), lambda n, m: (0, 0, 0)),
        ],
        out_specs=pl.BlockSpec((ATT_BLOCK, ATT_WIDTH), lambda n, m: (n, 0)),
        scratch_shapes=[
            pltpu.VMEM((N_Q_HEADS, ATT_BLOCK, HEAD_DIM), BF16),
            pltpu.VMEM((N_KV_HEADS, ATT_BLOCK + 2 * WINDOW, HEAD_DIM), BF16),
            pltpu.VMEM((N_KV_HEADS, ATT_BLOCK + 2 * WINDOW, HEAD_DIM), BF16),
            pltpu.VMEM((N_Q_HEADS, ATT_BLOCK, HEAD_DIM), BF16),
        ],
    )
    return pl.pallas_call(
        _attn_kernel,
        grid_spec=grid_spec,
        out_shape=jax.ShapeDtypeStruct((t, ATT_WIDTH), BF16),
        compiler_params=_cparams("parallel"),
        name="banded_attention",
    )(meta, proj, proj, proj, proj, proj, proj, proj, cs_table, cs_table, cs_table, sink_col)


def _gmlp_kernel(gu_ref, gz_ref, lng_ref, lnb_ref, ws_ref, bs_ref, o_ref, zn_scr):
    z = jax.nn.gelu(gz_ref[...].astype(F32))
    mu = jnp.mean(z, axis=-1, keepdims=True)
    zc = z - mu
    var = jnp.mean(zc * zc, axis=-1, keepdims=True)
    zn_scr[...] = (zc * lax.rsqrt(var + NORM_EPS) * lng_ref[...] + lnb_ref[...]).astype(BF16)
    gd = GM_WIDTH // GM_GROUPS
    for c in range(o_ref.shape[0] // GM_CHUNK):
        rs = slice(c * GM_CHUNK, (c + 1) * GM_CHUNK)
        for g in range(GM_GROUPS):
            cs = slice(g * gd, (g + 1) * gd)
            mixed = jnp.dot(ws_ref[g], zn_scr[rs, cs], preferred_element_type=F32) + bs_ref[g]
            u = jax.nn.gelu(gu_ref[rs, cs].astype(F32))
            o_ref[rs, cs] = (u * mixed).astype(BF16)


def spatial_gating(proj, ln_g, ln_b, w_s, b_s_b, *, bm):
    t = proj.shape[0]
    return pl.pallas_call(
        _gmlp_kernel,
        grid=(t // bm,),
        in_specs=[
            pl.BlockSpec((bm, GM_WIDTH), lambda i: (i, COL_GU)),
            pl.BlockSpec((bm, GM_WIDTH), lambda i: (i, COL_GZ)),
            pl.BlockSpec((1, GM_WIDTH), lambda i: (0, 0)),
            pl.BlockSpec((1, GM_WIDTH), lambda i: (0, 0)),
            pl.BlockSpec((GM_GROUPS, GM_CHUNK, GM_CHUNK), lambda i: (0, 0, 0)),
            pl.BlockSpec((GM_GROUPS, GM_CHUNK, GM_WIDTH // GM_GROUPS), lambda i: (0, 0, 0)),
        ],
        out_specs=pl.BlockSpec((bm, GM_WIDTH), lambda i: (i, 0)),
        out_shape=jax.ShapeDtypeStruct((t, GM_WIDTH), BF16),
        scratch_shapes=[pltpu.VMEM((bm, GM_WIDTH), BF16)],
        compiler_params=_cparams("parallel"),
        name="spatial_gating",
    )(proj, proj, ln_g.reshape(1, -1), ln_b.reshape(1, -1), w_s, b_s_b)


def _sigmoid(x):
    return 1.0 / (1.0 + jnp.exp(-x))


def _merge_kernel(att_ref, gm_ref, ga_ref, gb_ref, x_ref, wa_ref, wg_ref, wo_ref, y_ref):
    a = jnp.dot(att_ref[...], wa_ref[...], preferred_element_type=F32)
    b = jnp.dot(gm_ref[...], wg_ref[...], preferred_element_type=F32)
    merged = _sigmoid(ga_ref[...].astype(F32)) * a + _sigmoid(gb_ref[...].astype(F32)) * b
    y_ref[...] = x_ref[...] + jnp.dot(merged.astype(BF16), wo_ref[...], preferred_element_type=F32)


def merge_project(att, gm, proj, x, wa, wg, wo, *, bm):
    t, d = x.shape
    const = dict(pipeline_mode=pl.Buffered(1))
    return pl.pallas_call(
        _merge_kernel,
        grid=(t // bm,),
        in_specs=[
            pl.BlockSpec((bm, ATT_WIDTH), lambda i: (i, 0)),
            pl.BlockSpec((bm, GM_WIDTH), lambda i: (i, 0)),
            pl.BlockSpec((bm, d), lambda i: (i, COL_GA)),
            pl.BlockSpec((bm, d), lambda i: (i, COL_GB)),
            pl.BlockSpec((bm, d), lambda i: (i, 0)),
            pl.BlockSpec((ATT_WIDTH, d), lambda i: (0, 0), **const),
            pl.BlockSpec((GM_WIDTH, d), lambda i: (0, 0), **const),
            pl.BlockSpec((d, d), lambda i: (0, 0), **const),
        ],
        out_specs=pl.BlockSpec((bm, d), lambda i: (i, 0)),
        out_shape=jax.ShapeDtypeStruct((t, d), F32),
        compiler_params=_cparams("parallel"),
        name="merge_project",
    )(att, gm, proj, proj, x, wa, wg, wo)


def _cross_kernel(meta_ref, x_ref, g_ref, wq_ref, kv_ref, wo_ref, y_ref):
    x = x_ref[...]
    xn = _rms(x, g_ref[...]).astype(BF16)
    q = jnp.dot(xn, wq_ref[...], preferred_element_type=F32).astype(BF16)
    scale = MEM_HEAD_DIM ** -0.5
    outs = []
    for h in range(MEM_HEADS):
        sl = slice(h * MEM_HEAD_DIM, (h + 1) * MEM_HEAD_DIM)
        kh = kv_ref[0, :, sl]
        vh = kv_ref[0, :, MEM_WIDTH + h * MEM_HEAD_DIM:MEM_WIDTH + (h + 1) * MEM_HEAD_DIM]
        s = _dot_nt(q[:, sl], kh) * scale
        m = jnp.max(s, axis=-1, keepdims=True)
        e = jnp.exp(s - m)
        p = (e / jnp.sum(e, axis=-1, keepdims=True)).astype(BF16)
        outs.append(jnp.dot(p, vh, preferred_element_type=F32).astype(BF16))
    o = jnp.concatenate(outs, axis=-1)
    y_ref[...] = x + jnp.dot(o, wo_ref[...], preferred_element_type=F32)


def cross_attention(x, meta, g, wq, kv, wo, *, bm):
    t, d = x.shape
    grid_spec = pltpu.PrefetchScalarGridSpec(
        num_scalar_prefetch=1,
        grid=(t // bm,),
        in_specs=[
            pl.BlockSpec((bm, d), lambda i, m: (i, 0)),
            pl.BlockSpec((1, d), lambda i, m: (0, 0)),
            pl.BlockSpec((d, MEM_WIDTH), lambda i, m: (0, 0)),
            pl.BlockSpec((1, MEM_TOKENS, 2 * MEM_WIDTH), lambda i, m: (m[3, i], 0, 0)),
            pl.BlockSpec((MEM_WIDTH, d), lambda i, m: (0, 0)),
        ],
        out_specs=pl.BlockSpec((bm, d), lambda i, m: (i, 0)),
    )
    return pl.pallas_call(
        _cross_kernel,
        grid_spec=grid_spec,
        out_shape=jax.ShapeDtypeStruct((t, d), F32),
        compiler_params=_cparams("parallel"),
        name="cross_attention",
    )(meta, x, g.reshape(1, d), wq, kv, wo)


N_TOP = PEER_TOPK + 1
RANK_FAR = 2.0 * N_KEYS
CAND = [(a, b) for a in range(N_TOP) for b in range(N_TOP // (a + 1))]
CAND_ROWS = -(-len(CAND) // SUBLANES) * SUBLANES
XPOSE_COLS = 256


def _router_kernel(x_ref, g_ref, wqt_ref, k1_ref, k2_ref, xnt_ref, r2_ref, bz_ref, cnt_ref, a_ref,
                   qt_scr, c_scr):
    bm, d = x_ref.shape
    x = x_ref[...]
    inv = lax.rsqrt(jnp.mean(x * x, axis=-1, keepdims=True) + NORM_EPS)
    for c0 in range(0, d, XPOSE_COLS):
        cols = slice(c0, c0 + XPOSE_COLS)
        xn_c = x_ref[:, cols] * inv * g_ref[:, cols]
        xnt_ref[cols, :] = xn_c.T.astype(BF16)
    qt_scr[...] = jnp.dot(wqt_ref[...], xnt_ref[...],
                          preferred_element_type=F32).astype(BF16)
    c_scr[...] = jnp.full(c_scr.shape, NEG_INF, F32)

    def head(h, carry):
        off = pl.multiple_of(h * PEER_QDIM, PEER_QDIM)
        s1 = jnp.dot(k1_ref[...], qt_scr[pl.ds(off, PEER_HALF), :], preferred_element_type=F32)
        s2 = jnp.dot(k2_ref[...], qt_scr[pl.ds(off + PEER_HALF, PEER_HALF), :],
                     preferred_element_type=F32)
        v1, v2 = [], []
        t1, t2 = s1, s2
        rank2 = jnp.full(s2.shape, RANK_FAR, F32)
        for k in range(N_TOP):
            m1 = jnp.max(t1, axis=0, keepdims=True)
            m2 = jnp.max(t2, axis=0, keepdims=True)
            v1.append(m1)
            v2.append(m2)
            t1 = jnp.where(t1 == m1, NEG_INF, t1)
            hit2 = t2 == m2
            t2 = jnp.where(hit2, NEG_INF, t2)
            rank2 = jnp.where(hit2, float(k), rank2)
        for r, (a, b) in enumerate(CAND):
            c_scr[r:r + 1, :] = v1[a] + v2[b]
        c0 = c_scr[...]
        c = c0
        cum = jnp.zeros((1, bm), F32)
        t_in = jnp.full((1, bm), NEG_INF, F32)
        t_out = jnp.full((1, bm), NEG_INF, F32)
        for _ in range(N_TOP):
            m = jnp.max(c, axis=0, keepdims=True)
            hit = c == m
            cnt = jnp.sum(jnp.where(hit, 1.0, 0.0), axis=0, keepdims=True)
            t_in = jnp.where(cum < PEER_TOPK, m, t_in)
            t_out = jnp.where(cum >= PEER_TOPK, jnp.maximum(t_out, m), t_out)
            c = jnp.where(hit, NEG_INF, c)
            cum = cum + cnt
        thr = 0.5 * (t_in + t_out)
        top = v1[0] + v2[0]
        z = jnp.sum(jnp.where(c0 >= thr, jnp.exp(c0 - top), 0.0), axis=0, keepdims=True)
        d1 = thr - s1
        n_pair = jnp.zeros(s1.shape, F32)
        for b in range(PEER_TOPK):
            n_pair = n_pair + jnp.where(v2[b] >= d1, 1.0, 0.0)
        r2_ref[h] = rank2.astype(BF16)
        bz_ref[h] = (jnp.exp(s2 - v2[0]) / z).astype(BF16)
        cnt_ref[h] = n_pair
        a_ref[h] = jnp.exp(s1 - v1[0])
        return carry

    lax.fori_loop(0, PEER_HEADS, head, 0)


def peer_router(x, g, wqt, k1, k2, *, bm):
    t, d = x.shape
    head_spec = pl.BlockSpec((PEER_HEADS, N_KEYS, bm), lambda i: (0, 0, i))
    return pl.pallas_call(
        _router_kernel,
        grid=(t // bm,),
        in_specs=[
            pl.BlockSpec((bm, d), lambda i: (i, 0)),
            pl.BlockSpec((1, d), lambda i: (0, 0)),
            pl.BlockSpec((PEER_HEADS * PEER_QDIM, d), lambda i: (0, 0)),
            pl.BlockSpec((N_KEYS, PEER_HALF), lambda i: (0, 0)),
            pl.BlockSpec((N_KEYS, PEER_HALF), lambda i: (0, 0)),
        ],
        out_specs=[pl.BlockSpec((d, bm), lambda i: (0, i)), head_spec, head_spec, head_spec, head_spec],
        out_shape=[jax.ShapeDtypeStruct((d, t), BF16),
                   jax.ShapeDtypeStruct((PEER_HEADS, N_KEYS, t), BF16),
                   jax.ShapeDtypeStruct((PEER_HEADS, N_KEYS, t), BF16),
                   jax.ShapeDtypeStruct((PEER_HEADS, N_KEYS, t), F32),
                   jax.ShapeDtypeStruct((PEER_HEADS, N_KEYS, t), F32)],
        scratch_shapes=[pltpu.VMEM((PEER_HEADS * PEER_QDIM, bm), BF16), pltpu.VMEM((CAND_ROWS, bm), F32)],
        compiler_params=_cparams("parallel"),
        name="peer_router",
    )(x, g.reshape(1, d), wqt, k1, k2)


EXPERT_TILE = 1024
EXPERT_CHUNK = 256
KEYS_PER_TILE = EXPERT_TILE // N_KEYS
KEYS_PER_CHUNK = EXPERT_CHUNK // N_KEYS
CHUNKS_PER_TILE = EXPERT_TILE // EXPERT_CHUNK
TOKEN_GROUP = 2 * LANES
assert KEYS_PER_TILE == SUBLANES


def _expert_kernel(xnt_ref, u_ref, vt_ref, r2_ref, bz_ref, cnt_ref, a_ref, x_ref, y_ref, p_scr, acc_scr):
    e = pl.program_id(1)
    bm = x_ref.shape[0]

    @pl.when(e == 0)
    def _():
        acc_scr[...] = jnp.zeros(acc_scr.shape, F32)

    for c in range(CHUNKS_PER_TILE):
        crow = slice(c * EXPERT_CHUNK, (c + 1) * EXPERT_CHUNK)
        h = jnp.dot(u_ref[crow, :], xnt_ref[...], preferred_element_type=F32)
        for kk in range(KEYS_PER_CHUNK):
            il = c * KEYS_PER_CHUNK + kk
            rows = slice(il * N_KEYS, (il + 1) * N_KEYS)
            for tg in range(bm // TOKEN_GROUP):
                ts = slice(tg * TOKEN_GROUP, (tg + 1) * TOKEN_GROUP)
                g = jax.nn.gelu(h[kk * N_KEYS:(kk + 1) * N_KEYS, ts]).astype(BF16)
                p = jnp.zeros((N_KEYS, TOKEN_GROUP), BF16)
                for hd in range(PEER_HEADS):
                    cb = jnp.broadcast_to(cnt_ref[hd, il:il + 1, ts], (N_KEYS, TOKEN_GROUP)).astype(BF16)
                    ab = jnp.broadcast_to(a_ref[hd, il:il + 1, ts], (N_KEYS, TOKEN_GROUP)).astype(BF16)
                    p = p + jnp.where(r2_ref[hd, :, ts] < cb, bz_ref[hd, :, ts] * g, 0.0) * ab
                p_scr[rows, ts] = p
    acc_scr[...] += jnp.dot(vt_ref[...], p_scr[...], preferred_element_type=F32)

    @pl.when(e == pl.num_programs(1) - 1)
    def _():
        for c0 in range(0, y_ref.shape[1], XPOSE_COLS):
            cols = slice(c0, c0 + XPOSE_COLS)
            y_ref[:, cols] = x_ref[:, cols] + acc_scr[cols, :].T


def peer_experts(xnt, u, vt, r2, bz, cnt, a, x, *, bm):
    t, d = x.shape
    assert bm % TOKEN_GROUP == 0
    head_full = pl.BlockSpec((PEER_HEADS, N_KEYS, bm), lambda i, e: (0, 0, i))
    head_rows = pl.BlockSpec((PEER_HEADS, KEYS_PER_TILE, bm), lambda i, e: (0, e, i))
    return pl.pallas_call(
        _expert_kernel,
        grid=(t // bm, u.shape[0] // EXPERT_TILE),
        in_specs=[
            pl.BlockSpec((d, bm), lambda i, e: (0, i)),
            pl.BlockSpec((EXPERT_TILE, d), lambda i, e: (e, 0)),
            pl.BlockSpec((d, EXPERT_TILE), lambda i, e: (0, e)),
            head_full, head_full, head_rows, head_rows,
            pl.BlockSpec((bm, d), lambda i, e: (i, 0)),
        ],
        out_specs=pl.BlockSpec((bm, d), lambda i, e: (i, 0)),
        out_shape=jax.ShapeDtypeStruct((t, d), F32),
        scratch_shapes=[
            pltpu.VMEM((EXPERT_TILE, bm), BF16),
            pltpu.VMEM((d, bm), F32),
        ],
        compiler_params=_cparams("parallel", "arbitrary"),
        name="peer_experts",
    )(xnt, u, vt, r2, bz, cnt, a, x)


def _final_norm_kernel(x_ref, g_ref, y_ref):
    y_ref[...] = _rms(x_ref[...], g_ref[...])


def final_norm_slice(x, g, start, rows, *, bm):
    d = x.shape[1]
    off = start // bm
    return pl.pallas_call(
        _final_norm_kernel,
        grid=(rows // bm,),
        in_specs=[pl.BlockSpec((bm, d), lambda i: (i + off, 0)), pl.BlockSpec((1, d), lambda i: (0, 0))],
        out_specs=pl.BlockSpec((bm, d), lambda i: (i, 0)),
        out_shape=jax.ShapeDtypeStruct((rows, d), F32),
        compiler_params=_cparams("parallel"),
        name="final_norm",
    )(x, g.reshape(1, d))


def _block_meta(groups, blk):
    pos, first, last, batch = [], [], [], []
    b0 = 0
    for (b, s) in groups:
        per = s // blk
        for bi in range(b):
            for j in range(per):
                pos.append(j)
                first.append(int(j == 0))
                last.append(int(j == per - 1))
                batch.append(b0 + bi)
        b0 += b
    return jnp.asarray(np.array([pos, first, last, batch], dtype=np.int32))


def _rope_table(max_pos):
    half = HEAD_DIM // 2
    inv = ROPE_THETA ** (-jnp.arange(half, dtype=F32) / half)
    ang = jnp.arange(max_pos, dtype=F32)[:, None] * inv[None, :]
    cos, sin = jnp.cos(ang), jnp.sin(ang)
    return jnp.concatenate([cos, cos, -sin, sin], axis=-1)


def _pick(t, pref):
    while t % pref:
        pref //= 2
    return pref


def _trunk(xs, mems, mix_norm, w_in, attn_sink, gm_ln_g, gm_ln_b, gm_w_s, gm_b_s, w_att_proj, w_gm_proj,
           w_out, cross_norm, mem_norm, w_q_mem, w_kv_mem, w_o_mem, peer_norm, w_q_peer, peer_k1, peer_k2,
           expert_u, expert_v, final_norm):
    groups = [(x.shape[0], x.shape[1]) for x in xs]
    depth = w_in.shape[0]
    for (_, s) in groups:
        assert s % ATT_BLOCK == 0
    x = jnp.concatenate([x.reshape(-1, D_MODEL) for x in xs], axis=0)
    mem = jnp.concatenate([m.reshape(-1, D_MODEL) for m in mems], axis=0)
    t = x.shape[0]
    n_mem_rows = mem.shape[0]
    meta = _block_meta(groups, ATT_BLOCK)
    cs_table = _rope_table(max(s for _, s in groups))

    bm_big = _pick(t, 1024)
    bm_mid = _pick(t, 512)
    bm_small = _pick(t, 256)
    in_splits = np.cumsum([ATT_WIDTH, KV_WIDTH, KV_WIDTH, GM_WIDTH, GM_WIDTH, D_MODEL]).tolist()

    for l in range(depth):
        q_w, k_w, v_w, gu_w, gz_w, ga_w, gb_w = jnp.split(w_in[l], in_splits, axis=1)
        w_in_p = jnp.concatenate([q_w, gu_w, gz_w, ga_w, gb_w, k_w, v_w], axis=1).astype(BF16)
        sink_col = jnp.broadcast_to(attn_sink[l].astype(F32)[:, None, None], (N_Q_HEADS, WINDOW, SINK_LANES))
        b_s_b = jnp.broadcast_to(gm_b_s[l].astype(F32)[:, :, None],
                                 (GM_GROUPS, GM_CHUNK, GM_WIDTH // GM_GROUPS))

        proj = norm_matmul(x, mix_norm[l], w_in_p, bm=bm_big, bn=1024)
        att = banded_attention(proj, meta, cs_table, sink_col)
        gm = spatial_gating(proj, gm_ln_g[l], gm_ln_b[l], gm_w_s[l].astype(BF16), b_s_b, bm=ATT_BLOCK)
        x = merge_project(att, gm, proj, x, w_att_proj[l].astype(BF16), w_gm_proj[l].astype(BF16),
                          w_out[l].astype(BF16), bm=bm_small)

        kv = norm_matmul(mem, mem_norm[l], w_kv_mem[l].astype(BF16), bm=_pick(n_mem_rows, 512), bn=1024)
        kv = kv.reshape(-1, MEM_TOKENS, 2 * MEM_WIDTH)
        x = cross_attention(x, meta, cross_norm[l], w_q_mem[l].astype(BF16), kv, w_o_mem[l].astype(BF16),
                            bm=ATT_BLOCK)

        xnt, r2, bz, cnt, a = peer_router(x, peer_norm[l], w_q_peer[l].T.astype(BF16),
                                          peer_k1[l].astype(BF16), peer_k2[l].astype(BF16), bm=bm_mid)
        x = peer_experts(xnt, expert_u[l].astype(BF16), expert_v[l].T.astype(BF16), r2, bz, cnt, a, x,
                         bm=bm_mid)

    outs = []
    start = 0
    for (b, s) in groups:
        rows = b * s
        y = final_norm_slice(x, final_norm, start, rows, bm=_pick(int(np.gcd(rows, start)), 1024))
        outs.append(y.reshape(b, s, D_MODEL))
        start += rows
    return tuple(outs)


def kernel(x_prompt, x_sample, mem_prompt, mem_sample, mix_norm, w_in, attn_sink, gm_ln_g, gm_ln_b, gm_w_s, gm_b_s, w_att_proj, w_gm_proj, w_out, cross_norm, mem_norm, w_q_mem, w_kv_mem, w_o_mem, peer_norm, w_q_peer, peer_k1, peer_k2, expert_u, expert_v, final_norm):
    return _trunk((x_prompt, x_sample), (mem_prompt, mem_sample), mix_norm, w_in, attn_sink, gm_ln_g,
                  gm_ln_b, gm_w_s, gm_b_s, w_att_proj, w_gm_proj, w_out, cross_norm, mem_norm, w_q_mem,
                  w_kv_mem, w_o_mem, peer_norm, w_q_peer, peer_k1, peer_k2, expert_u, expert_v, final_norm)
```

```python
import numpy as np
import jax
import jax.numpy as jnp
from jax import lax
from jax.experimental import pallas as pl
from jax.experimental.pallas import tpu as pltpu

F32 = jnp.float32
BF16 = jnp.bfloat16

D_MODEL = 2048
HEAD_DIM = 128
N_Q_HEADS = 16
N_KV_HEADS = 4
Q_PER_KV = N_Q_HEADS // N_KV_HEADS
ATT_WIDTH = N_Q_HEADS * HEAD_DIM
KV_WIDTH = N_KV_HEADS * HEAD_DIM
WINDOW = 128
ROPE_THETA = 10000.0
GM_WIDTH = 2048
GM_GROUPS = 16
GM_CHUNK = 128
MEM_TOKENS = 256
MEM_HEADS = 4
MEM_HEAD_DIM = 128
MEM_WIDTH = MEM_HEADS * MEM_HEAD_DIM
PEER_HEADS = 8
PEER_QDIM = 256
PEER_HALF = PEER_QDIM // 2
N_KEYS = 128
N_EXPERTS = N_KEYS * N_KEYS
PEER_TOPK = 16
NORM_EPS = 1e-6
NEG_INF = -1e30

COL_Q, COL_GU, COL_GZ, COL_GA, COL_GB = 0, 1, 2, 3, 4
COL_K = (ATT_WIDTH + 4 * D_MODEL) // KV_WIDTH
COL_V = COL_K + 1

LANES = 128
SUBLANES = 8
ATT_BLOCK = 512
SINK_LANES = LANES
VMEM_LIMIT = 56 * 1024 * 1024


def _cparams(*sem):
    return pltpu.CompilerParams(dimension_semantics=sem, vmem_limit_bytes=VMEM_LIMIT)


def _rms(x, g):
    return x * lax.rsqrt(jnp.mean(x * x, axis=-1, keepdims=True) + NORM_EPS) * g


def _dot_nt(a, b):
    return lax.dot_general(a, b, (((1,), (1,)), ((), ())), preferred_element_type=F32)


def _norm_matmul_kernel(x_ref, g_ref, w_ref, o_ref, xn_ref):
    @pl.when(pl.program_id(1) == 0)
    def _():
        xn_ref[...] = _rms(x_ref[...], g_ref[...]).astype(BF16)

    o_ref[...] = jnp.dot(xn_ref[...], w_ref[...], preferred_element_type=F32).astype(o_ref.dtype)


def norm_matmul(x, g, w, *, bm, bn):
    t, d = x.shape
    n = w.shape[1]
    assert t % bm == 0 and n % bn == 0
    return pl.pallas_call(
        _norm_matmul_kernel,
        grid=(t // bm, n // bn),
        in_specs=[
            pl.BlockSpec((bm, d), lambda i, j: (i, 0)),
            pl.BlockSpec((1, d), lambda i, j: (0, 0)),
            pl.BlockSpec((d, bn), lambda i, j: (0, j)),
        ],
        out_specs=pl.BlockSpec((bm, bn), lambda i, j: (i, j)),
        out_shape=jax.ShapeDtypeStruct((t, n), BF16),
        scratch_shapes=[pltpu.VMEM((bm, d), BF16)],
        compiler_params=_cparams("parallel", "arbitrary"),
        name="norm_matmul",
    )(x, g.reshape(1, d), w)


def _attn_kernel(meta_ref, q_ref, kc_ref, kp_ref, kn_ref, vc_ref, vp_ref, vn_ref,
                 csc_ref, csp_ref, csn_ref, sink_ref, o_ref, q_scr, k_scr, v_scr, o_scr):
    n = pl.program_id(0)
    nb = ATT_BLOCK // WINDOW
    band = 3 * WINDOW
    k_lo = jnp.where(meta_ref[1, n] == 1, WINDOW, 0)
    k_hi = jnp.where(meta_ref[2, n] == 1, 2 * WINDOW, band)

    def rope(x, cs):
        return x * cs[:, :HEAD_DIM] + pltpu.roll(x, HEAD_DIM // 2, 1) * cs[:, HEAD_DIM:]

    cs_c = csc_ref[...]
    for hq in range(N_Q_HEADS):
        sl = slice(hq * HEAD_DIM, (hq + 1) * HEAD_DIM)
        q_scr[hq] = rope(q_ref[:, sl].astype(F32), cs_c).astype(BF16)
    cs_p = csp_ref[...]
    cs_n = csn_ref[...]
    for h in range(N_KV_HEADS):
        sl = slice(h * HEAD_DIM, (h + 1) * HEAD_DIM)
        k_scr[h, 0:WINDOW, :] = rope(kp_ref[:, sl].astype(F32), cs_p).astype(BF16)
        k_scr[h, WINDOW:WINDOW + ATT_BLOCK, :] = rope(kc_ref[:, sl].astype(F32), cs_c).astype(BF16)
        k_scr[h, WINDOW + ATT_BLOCK:, :] = rope(kn_ref[:, sl].astype(F32), cs_n).astype(BF16)
        v_scr[h, 0:WINDOW, :] = vp_ref[:, sl]
        v_scr[h, WINDOW:WINDOW + ATT_BLOCK, :] = vc_ref[:, sl]
        v_scr[h, WINDOW + ATT_BLOCK:, :] = vn_ref[:, sl]

    qi = lax.broadcasted_iota(jnp.int32, (WINDOW, band), 0)
    kc = lax.broadcasted_iota(jnp.int32, (WINDOW, band), 1)
    rel = jnp.abs(qi + WINDOW - kc) <= WINDOW
    scale = HEAD_DIM ** -0.5

    def band_head(idx, carry):
        r = idx // N_KV_HEADS
        h = idx % N_KV_HEADS
        r0 = pl.multiple_of(r * WINDOW, WINDOW)
        lo = jnp.where(r == 0, k_lo, 0)
        hi = jnp.where(r == nb - 1, k_hi, band)
        bias = jnp.where(jnp.logical_and(rel, jnp.logical_and(kc >= lo, kc < hi)), 0.0, NEG_INF)
        kb = k_scr[h, pl.ds(r0, band), :]
        vb = v_scr[h, pl.ds(r0, band), :]
        scores = [_dot_nt(q_scr[Q_PER_KV * h + g, pl.ds(r0, WINDOW), :], kb) for g in range(Q_PER_KV)]
        for g in range(Q_PER_KV):
            hq = Q_PER_KV * h + g
            s = scores[g] * scale + bias
            sk = sink_ref[hq]
            m = jnp.maximum(jnp.max(s, axis=-1, keepdims=True), sk)
            e = jnp.concatenate([jnp.exp(s[:, j * WINDOW:(j + 1) * WINDOW] - m) for j in range(3)], axis=-1)
            den = jnp.sum(e, axis=-1, keepdims=True) + jnp.exp(sk - m)
            o = jnp.dot(e.astype(BF16), vb, preferred_element_type=F32) * (1.0 / den)
            o_scr[hq, pl.ds(r0, WINDOW), :] = o.astype(BF16)
        return carry

    lax.fori_loop(0, nb * N_KV_HEADS, band_head, 0)
    for hq in range(N_Q_HEADS):
        o_ref[:, hq * HEAD_DIM:(hq + 1) * HEAD_DIM] = o_scr[hq]


def banded_attention(proj, meta, cs_table, sink_col):
    t = proj.shape[0]
    nblk = t // ATT_BLOCK
    sub = ATT_BLOCK // WINDOW
    n_small = t // WINDOW
    n_pos_small = cs_table.shape[0] // WINDOW
    grid_spec = pltpu.PrefetchScalarGridSpec(
        num_scalar_prefetch=1,
        grid=(nblk,),
        in_specs=[
            pl.BlockSpec((ATT_BLOCK, ATT_WIDTH), lambda n, m: (n, COL_Q)),
            pl.BlockSpec((ATT_BLOCK, KV_WIDTH), lambda n, m: (n, COL_K)),
            pl.BlockSpec((WINDOW, KV_WIDTH), lambda n, m: (jnp.maximum(n * sub - 1, 0), COL_K)),
            pl.BlockSpec((WINDOW, KV_WIDTH), lambda n, m: (jnp.minimum(n * sub + sub, n_small - 1), COL_K)),
            pl.BlockSpec((ATT_BLOCK, KV_WIDTH), lambda n, m: (n, COL_V)),
            pl.BlockSpec((WINDOW, KV_WIDTH), lambda n, m: (jnp.maximum(n * sub - 1, 0), COL_V)),
            pl.BlockSpec((WINDOW, KV_WIDTH), lambda n, m: (jnp.minimum(n * sub + sub, n_small - 1), COL_V)),
            pl.BlockSpec((ATT_BLOCK, 2 * HEAD_DIM), lambda n, m: (m[0, n], 0)),
            pl.BlockSpec((WINDOW, 2 * HEAD_DIM), lambda n, m: (jnp.maximum(m[0, n] * sub - 1, 0), 0)),
            pl.BlockSpec((WINDOW, 2 * HEAD_DIM),
                         lambda n, m: (jnp.minimum(m[0, n] * sub + sub, n_pos_small - 1), 0)),
            pl.BlockSpec((N_Q_HEADS, WINDOW, SINK_LANES), lambda n, m: (0, 0, 0)),
        ],
        out_specs=pl.BlockSpec((ATT_BLOCK, ATT_WIDTH), lambda n, m: (n, 0)),
        scratch_shapes=[
            pltpu.VMEM((N_Q_HEADS, ATT_BLOCK, HEAD_DIM), BF16),
            pltpu.VMEM((N_KV_HEADS, ATT_BLOCK + 2 * WINDOW, HEAD_DIM), BF16),
            pltpu.VMEM((N_KV_HEADS, ATT_BLOCK + 2 * WINDOW, HEAD_DIM), BF16),
            pltpu.VMEM((N_Q_HEADS, ATT_BLOCK, HEAD_DIM), BF16),
        ],
    )
    return pl.pallas_call(
        _attn_kernel,
        grid_spec=grid_spec,
        out_shape=jax.ShapeDtypeStruct((t, ATT_WIDTH), BF16),
        compiler_params=_cparams("parallel"),
        name="banded_attention",
    )(meta, proj, proj, proj, proj, proj, proj, proj, cs_table, cs_table, cs_table, sink_col)


def _gmlp_kernel(gu_ref, gz_ref, lng_ref, lnb_ref, ws_ref, bs_ref, o_ref, zn_scr, u_scr):
    z = jax.nn.gelu(gz_ref[...].astype(F32))
    mu = jnp.mean(z, axis=-1, keepdims=True)
    zc = z - mu
    var = jnp.mean(zc * zc, axis=-1, keepdims=True)
    zn_scr[...] = (zc * lax.rsqrt(var + NORM_EPS) * lng_ref[...] + lnb_ref[...]).astype(BF16)
    u_scr[...] = jax.nn.gelu(gu_ref[...])
    gd = GM_WIDTH // GM_GROUPS
    for c in range(o_ref.shape[0] // GM_CHUNK):
        rs = slice(c * GM_CHUNK, (c + 1) * GM_CHUNK)
        for g in range(GM_GROUPS):
            cs = slice(g * gd, (g + 1) * gd)
            mixed = jnp.dot(ws_ref[g], zn_scr[rs, cs], preferred_element_type=F32) + bs_ref[g]
            o_ref[rs, cs] = (u_scr[rs, cs].astype(F32) * mixed).astype(BF16)


def spatial_gating(proj, ln_g, ln_b, w_s, b_s_b, *, bm):
    t = proj.shape[0]
    return pl.pallas_call(
        _gmlp_kernel,
        grid=(t // bm,),
        in_specs=[
            pl.BlockSpec((bm, GM_WIDTH), lambda i: (i, COL_GU)),
            pl.BlockSpec((bm, GM_WIDTH), lambda i: (i, COL_GZ)),
            pl.BlockSpec((1, GM_WIDTH), lambda i: (0, 0)),
            pl.BlockSpec((1, GM_WIDTH), lambda i: (0, 0)),
            pl.BlockSpec((GM_GROUPS, GM_CHUNK, GM_CHUNK), lambda i: (0, 0, 0)),
            pl.BlockSpec((GM_GROUPS, GM_CHUNK, GM_WIDTH // GM_GROUPS), lambda i: (0, 0, 0)),
        ],
        out_specs=pl.BlockSpec((bm, GM_WIDTH), lambda i: (i, 0)),
        out_shape=jax.ShapeDtypeStruct((t, GM_WIDTH), BF16),
        scratch_shapes=[pltpu.VMEM((bm, GM_WIDTH), BF16), pltpu.VMEM((bm, GM_WIDTH), BF16)],
        compiler_params=_cparams("parallel"),
        name="spatial_gating",
    )(proj, proj, ln_g.reshape(1, -1), ln_b.reshape(1, -1), w_s, b_s_b)


def _sigmoid(x):
    return 1.0 / (1.0 + jnp.exp(-x))


def _merge_kernel(att_ref, gm_ref, ga_ref, gb_ref, x_ref, wa_ref, wg_ref, wo_ref, y_ref):
    a = jnp.dot(att_ref[...], wa_ref[...], preferred_element_type=F32)
    b = jnp.dot(gm_ref[...], wg_ref[...], preferred_element_type=F32)
    merged = _sigmoid(ga_ref[...].astype(F32)) * a + _sigmoid(gb_ref[...].astype(F32)) * b
    y_ref[...] = x_ref[...] + jnp.dot(merged.astype(BF16), wo_ref[...], preferred_element_type=F32)


def merge_project(att, gm, proj, x, wa, wg, wo, *, bm):
    t, d = x.shape
    const = dict(pipeline_mode=pl.Buffered(1))
    return pl.pallas_call(
        _merge_kernel,
        grid=(t // bm,),
        in_specs=[
            pl.BlockSpec((bm, ATT_WIDTH), lambda i: (i, 0)),
            pl.BlockSpec((bm, GM_WIDTH), lambda i: (i, 0)),
            pl.BlockSpec((bm, d), lambda i: (i, COL_GA)),
            pl.BlockSpec((bm, d), lambda i: (i, COL_GB)),
            pl.BlockSpec((bm, d), lambda i: (i, 0)),
            pl.BlockSpec((ATT_WIDTH, d), lambda i: (0, 0), **const),
            pl.BlockSpec((GM_WIDTH, d), lambda i: (0, 0), **const),
            pl.BlockSpec((d, d), lambda i: (0, 0), **const),
        ],
        out_specs=pl.BlockSpec((bm, d), lambda i: (i, 0)),
        out_shape=jax.ShapeDtypeStruct((t, d), F32),
        compiler_params=_cparams("parallel"),
        name="merge_project",
    )(att, gm, proj, proj, x, wa, wg, wo)


def _cross_kernel(meta_ref, x_ref, g_ref, wq_ref, kv_ref, wo_ref, y_ref):
    x = x_ref[...]
    xn = _rms(x, g_ref[...]).astype(BF16)
    q = jnp.dot(xn, wq_ref[...], preferred_element_type=F32).astype(BF16)
    scale = MEM_HEAD_DIM ** -0.5
    outs = []
    for h in range(MEM_HEADS):
        sl = slice(h * MEM_HEAD_DIM, (h + 1) * MEM_HEAD_DIM)
        kh = kv_ref[0, :, sl]
        vh = kv_ref[0, :, MEM_WIDTH + h * MEM_HEAD_DIM:MEM_WIDTH + (h + 1) * MEM_HEAD_DIM]
        s = _dot_nt(q[:, sl], kh) * scale
        m = jnp.max(s, axis=-1, keepdims=True)
        e = jnp.exp(s - m)
        p = (e / jnp.sum(e, axis=-1, keepdims=True)).astype(BF16)
        outs.append(jnp.dot(p, vh, preferred_element_type=F32).astype(BF16))
    o = jnp.concatenate(outs, axis=-1)
    y_ref[...] = x + jnp.dot(o, wo_ref[...], preferred_element_type=F32)


def cross_attention(x, meta, g, wq, kv, wo, *, bm):
    t, d = x.shape
    grid_spec = pltpu.PrefetchScalarGridSpec(
        num_scalar_prefetch=1,
        grid=(t // bm,),
        in_specs=[
            pl.BlockSpec((bm, d), lambda i, m: (i, 0)),
            pl.BlockSpec((1, d), lambda i, m: (0, 0)),
            pl.BlockSpec((d, MEM_WIDTH), lambda i, m: (0, 0)),
            pl.BlockSpec((1, MEM_TOKENS, 2 * MEM_WIDTH), lambda i, m: (m[3, i], 0, 0)),
            pl.BlockSpec((MEM_WIDTH, d), lambda i, m: (0, 0)),
        ],
        out_specs=pl.BlockSpec((bm, d), lambda i, m: (i, 0)),
    )
    return pl.pallas_call(
        _cross_kernel,
        grid_spec=grid_spec,
        out_shape=jax.ShapeDtypeStruct((t, d), F32),
        compiler_params=_cparams("parallel"),
        name="cross_attention",
    )(meta, x, g.reshape(1, d), wq, kv, wo)


N_TOP = PEER_TOPK + 1
CAND = [(a, b) for a in range(N_TOP) for b in range(N_TOP // (a + 1))]
CAND_ROWS = 8 * SUBLANES
assert len(CAND) <= CAND_ROWS
XPOSE_COLS = 256


def _oddeven_merge_sort(n):
    def merge(lo, hi, r):
        step = r * 2
        if step < hi - lo:
            yield from merge(lo, hi, step)
            yield from merge(lo + r, hi, step)
            yield from [(i, i + r) for i in range(lo + r, hi - r, step)]
        else:
            yield (lo, lo + r)

    def sort(lo, hi):
        if hi - lo >= 1:
            mid = lo + (hi - lo) // 2
            yield from sort(lo, mid)
            yield from sort(mid + 1, hi)
            yield from merge(lo, hi, 1)

    return list(sort(0, n - 1))


def _top_values(s, n_top, one_at_a_time=False):
    nblk = s.shape[0] // SUBLANES
    v = [s[d * SUBLANES:(d + 1) * SUBLANES, :] for d in range(nblk)]
    for a, b in _oddeven_merge_sort(nblk):
        v[a], v[b] = jnp.maximum(v[a], v[b]), jnp.minimum(v[a], v[b])
    sub = lax.broadcasted_iota(jnp.int32, v[0].shape, 0)
    vals = []
    for k in range(n_top):
        m = jnp.max(v[0], axis=0, keepdims=True)
        vals.append(m)
        remaining = n_top - 1 - k
        if remaining == 0:
            break
        hit = v[0] == m
        if one_at_a_time:
            first = jnp.min(jnp.where(hit, sub, SUBLANES), axis=0, keepdims=True)
            hit = sub == first
        for d in range(min(nblk, remaining)):
            v[d] = jnp.where(hit, v[d + 1] if d + 1 < nblk else NEG_INF, v[d])
    return vals


def _count_above(vals, x, strict):
    assert len(vals) == 16

    def above(v):
        return (v > x) if strict else (v >= x)

    bits = []

    def pivot(level, lo, size, depth=0):
        mid = lo + size // 2
        if depth == level:
            return vals[mid]
        return jnp.where(bits[depth], pivot(level, mid + 1, size // 2, depth + 1),
                         pivot(level, lo, size // 2, depth + 1))

    count = jnp.where(above(vals[15]), 1.0, 0.0)
    for level in range(4):
        bits.append(above(pivot(level, 0, 15)))
        count = count + jnp.where(bits[level], float(8 >> level), 0.0)
    return count


def _router_kernel(x_ref, g_ref, wqt_ref, k1_ref, k2_ref, xnt_ref, r2_ref, bz_ref, cnt_ref, a_ref,
                   qt_scr, c_scr):
    d = x_ref.shape[1]
    x = x_ref[...]
    inv = lax.rsqrt(jnp.mean(x * x, axis=-1, keepdims=True) + NORM_EPS)
    for c0 in range(0, d, XPOSE_COLS):
        cols = slice(c0, c0 + XPOSE_COLS)
        xn_c = x_ref[:, cols] * inv * g_ref[:, cols]
        xnt_ref[cols, :] = xn_c.T.astype(BF16)
    qt_scr[...] = jnp.dot(wqt_ref[...], xnt_ref[...],
                          preferred_element_type=F32).astype(BF16)
    c_scr[...] = jnp.full(c_scr.shape, NEG_INF, F32)

    def head(h, carry):
        off = pl.multiple_of(h * PEER_QDIM, PEER_QDIM)
        s1 = jnp.dot(k1_ref[...], qt_scr[pl.ds(off, PEER_HALF), :], preferred_element_type=F32)
        s2 = jnp.dot(k2_ref[...], qt_scr[pl.ds(off + PEER_HALF, PEER_HALF), :],
                     preferred_element_type=F32)
        v1 = _top_values(s1, N_TOP)
        v2 = _top_values(s2, N_TOP)
        for r, (a, b) in enumerate(CAND):
            c_scr[r:r + 1, :] = v1[a] + v2[b]
        c0 = c_scr[...]
        csort = _top_values(c0, N_TOP, one_at_a_time=True)
        thr = 0.5 * (csort[PEER_TOPK - 1] + csort[PEER_TOPK])
        top = v1[0] + v2[0]
        z = jnp.sum(jnp.where(c0 >= thr, jnp.exp(c0 - top), 0.0), axis=0, keepdims=True)
        r2_ref[h] = _count_above(v2[:PEER_TOPK], s2, strict=True).astype(BF16)
        bz_ref[h] = (jnp.exp(s2 - v2[0]) / z).astype(BF16)
        cnt_ref[h] = _count_above(v2[:PEER_TOPK], thr - s1, strict=False)
        a_ref[h] = jnp.exp(s1 - v1[0])
        return carry

    lax.fori_loop(0, PEER_HEADS, head, 0)


def peer_router(x, g, wqt, k1, k2, *, bm):
    t, d = x.shape
    head_spec = pl.BlockSpec((PEER_HEADS, N_KEYS, bm), lambda i: (0, 0, i))
    return pl.pallas_call(
        _router_kernel,
        grid=(t // bm,),
        in_specs=[
            pl.BlockSpec((bm, d), lambda i: (i, 0)),
            pl.BlockSpec((1, d), lambda i: (0, 0)),
            pl.BlockSpec((PEER_HEADS * PEER_QDIM, d), lambda i: (0, 0)),
            pl.BlockSpec((N_KEYS, PEER_HALF), lambda i: (0, 0)),
            pl.BlockSpec((N_KEYS, PEER_HALF), lambda i: (0, 0)),
        ],
        out_specs=[pl.BlockSpec((d, bm), lambda i: (0, i)), head_spec, head_spec, head_spec, head_spec],
        out_shape=[jax.ShapeDtypeStruct((d, t), BF16),
                   jax.ShapeDtypeStruct((PEER_HEADS, N_KEYS, t), BF16),
                   jax.ShapeDtypeStruct((PEER_HEADS, N_KEYS, t), BF16),
                   jax.ShapeDtypeStruct((PEER_HEADS, N_KEYS, t), F32),
                   jax.ShapeDtypeStruct((PEER_HEADS, N_KEYS, t), F32)],
        scratch_shapes=[pltpu.VMEM((PEER_HEADS * PEER_QDIM, bm), BF16), pltpu.VMEM((CAND_ROWS, bm), F32)],
        compiler_params=_cparams("parallel"),
        name="peer_router",
    )(x, g.reshape(1, d), wqt, k1, k2)


EXPERT_TILE = 1024
EXPERT_CHUNK = 256
KEYS_PER_TILE = EXPERT_TILE // N_KEYS
KEYS_PER_CHUNK = EXPERT_CHUNK // N_KEYS
CHUNKS_PER_TILE = EXPERT_TILE // EXPERT_CHUNK
TOKEN_GROUP = 2 * LANES
assert KEYS_PER_TILE == SUBLANES


def _expert_kernel(xnt_ref, u_ref, vt_ref, r2_ref, bz_ref, cnt_ref, a_ref, x_ref, y_ref, p_scr, acc_scr):
    e = pl.program_id(1)
    bm = x_ref.shape[0]

    @pl.when(e == 0)
    def _():
        acc_scr[...] = jnp.zeros(acc_scr.shape, F32)

    for c in range(CHUNKS_PER_TILE):
        crow = slice(c * EXPERT_CHUNK, (c + 1) * EXPERT_CHUNK)
        h = jnp.dot(u_ref[crow, :], xnt_ref[...], preferred_element_type=F32)
        for kk in range(KEYS_PER_CHUNK):
            il = c * KEYS_PER_CHUNK + kk
            rows = slice(il * N_KEYS, (il + 1) * N_KEYS)
            for tg in range(bm // TOKEN_GROUP):
                ts = slice(tg * TOKEN_GROUP, (tg + 1) * TOKEN_GROUP)
                g = jax.nn.gelu(h[kk * N_KEYS:(kk + 1) * N_KEYS, ts].astype(BF16))
                p = jnp.zeros((N_KEYS, TOKEN_GROUP), BF16)
                for hd in range(PEER_HEADS):
                    cb = jnp.broadcast_to(cnt_ref[hd, il:il + 1, ts], (N_KEYS, TOKEN_GROUP)).astype(BF16)
                    ab = jnp.broadcast_to(a_ref[hd, il:il + 1, ts], (N_KEYS, TOKEN_GROUP)).astype(BF16)
                    p = p + jnp.where(r2_ref[hd, :, ts] < cb, bz_ref[hd, :, ts] * g, 0.0) * ab
                p_scr[rows, ts] = p
    acc_scr[...] += jnp.dot(vt_ref[...], p_scr[...], preferred_element_type=F32)

    @pl.when(e == pl.num_programs(1) - 1)
    def _():
        for c0 in range(0, y_ref.shape[1], XPOSE_COLS):
            cols = slice(c0, c0 + XPOSE_COLS)
            y_ref[:, cols] = x_ref[:, cols] + acc_scr[cols, :].T


def peer_experts(xnt, u, vt, r2, bz, cnt, a, x, *, bm):
    t, d = x.shape
    assert bm % TOKEN_GROUP == 0
    head_full = pl.BlockSpec((PEER_HEADS, N_KEYS, bm), lambda i, e: (0, 0, i))
    head_rows = pl.BlockSpec((PEER_HEADS, KEYS_PER_TILE, bm), lambda i, e: (0, e, i))
    return pl.pallas_call(
        _expert_kernel,
        grid=(t // bm, u.shape[0] // EXPERT_TILE),
        in_specs=[
            pl.BlockSpec((d, bm), lambda i, e: (0, i)),
            pl.BlockSpec((EXPERT_TILE, d), lambda i, e: (e, 0)),
            pl.BlockSpec((d, EXPERT_TILE), lambda i, e: (0, e)),
            head_full, head_full, head_rows, head_rows,
            pl.BlockSpec((bm, d), lambda i, e: (i, 0)),
        ],
        out_specs=pl.BlockSpec((bm, d), lambda i, e: (i, 0)),
        out_shape=jax.ShapeDtypeStruct((t, d), F32),
        scratch_shapes=[
            pltpu.VMEM((EXPERT_TILE, bm), BF16),
            pltpu.VMEM((d, bm), F32),
        ],
        compiler_params=_cparams("parallel", "arbitrary"),
        name="peer_experts",
    )(xnt, u, vt, r2, bz, cnt, a, x)


def _final_norm_kernel(x_ref, g_ref, y_ref):
    y_ref[...] = _rms(x_ref[...], g_ref[...])


def final_norm_slice(x, g, start, rows, *, bm):
    d = x.shape[1]
    off = start // bm
    return pl.pallas_call(
        _final_norm_kernel,
        grid=(rows // bm,),
        in_specs=[pl.BlockSpec((bm, d), lambda i: (i + off, 0)), pl.BlockSpec((1, d), lambda i: (0, 0))],
        out_specs=pl.BlockSpec((bm, d), lambda i: (i, 0)),
        out_shape=jax.ShapeDtypeStruct((rows, d), F32),
        compiler_params=_cparams("parallel"),
        name="final_norm",
    )(x, g.reshape(1, d))


def _block_meta(groups, blk):
    pos, first, last, batch = [], [], [], []
    b0 = 0
    for (b, s) in groups:
        per = s // blk
        for bi in range(b):
            for j in range(per):
                pos.append(j)
                first.append(int(j == 0))
                last.append(int(j == per - 1))
                batch.append(b0 + bi)
        b0 += b
    return jnp.asarray(np.array([pos, first, last, batch], dtype=np.int32))


def _rope_table(max_pos):
    half = HEAD_DIM // 2
    inv = ROPE_THETA ** (-jnp.arange(half, dtype=F32) / half)
    ang = jnp.arange(max_pos, dtype=F32)[:, None] * inv[None, :]
    cos, sin = jnp.cos(ang), jnp.sin(ang)
    return jnp.concatenate([cos, cos, -sin, sin], axis=-1)


def _pick(t, pref):
    while t % pref:
        pref //= 2
    return pref


def _trunk(xs, mems, mix_norm, w_in, attn_sink, gm_ln_g, gm_ln_b, gm_w_s, gm_b_s, w_att_proj, w_gm_proj,
           w_out, cross_norm, mem_norm, w_q_mem, w_kv_mem, w_o_mem, peer_norm, w_q_peer, peer_k1, peer_k2,
           expert_u, expert_v, final_norm):
    groups = [(x.shape[0], x.shape[1]) for x in xs]
    depth = w_in.shape[0]
    for (_, s) in groups:
        assert s % ATT_BLOCK == 0
    x = jnp.concatenate([x.reshape(-1, D_MODEL) for x in xs], axis=0)
    mem = jnp.concatenate([m.reshape(-1, D_MODEL) for m in mems], axis=0)
    t = x.shape[0]
    n_mem_rows = mem.shape[0]
    meta = _block_meta(groups, ATT_BLOCK)
    cs_table = _rope_table(max(s for _, s in groups))

    bm_big = _pick(t, 1024)
    bm_mid = _pick(t, 512)
    bm_small = _pick(t, 256)
    in_splits = np.cumsum([ATT_WIDTH, KV_WIDTH, KV_WIDTH, GM_WIDTH, GM_WIDTH, D_MODEL]).tolist()

    for l in range(depth):
        q_w, k_w, v_w, gu_w, gz_w, ga_w, gb_w = jnp.split(w_in[l], in_splits, axis=1)
        w_in_p = jnp.concatenate([q_w, gu_w, gz_w, ga_w, gb_w, k_w, v_w], axis=1).astype(BF16)
        sink_col = jnp.broadcast_to(attn_sink[l].astype(F32)[:, None, None], (N_Q_HEADS, WINDOW, SINK_LANES))
        b_s_b = jnp.broadcast_to(gm_b_s[l].astype(F32)[:, :, None],
                                 (GM_GROUPS, GM_CHUNK, GM_WIDTH // GM_GROUPS))

        proj = norm_matmul(x, mix_norm[l], w_in_p, bm=bm_big, bn=1024)
        att = banded_attention(proj, meta, cs_table, sink_col)
        gm = spatial_gating(proj, gm_ln_g[l], gm_ln_b[l], gm_w_s[l].astype(BF16), b_s_b, bm=ATT_BLOCK)
        x = merge_project(att, gm, proj, x, w_att_proj[l].astype(BF16), w_gm_proj[l].astype(BF16),
                          w_out[l].astype(BF16), bm=bm_small)

        kv = norm_matmul(mem, mem_norm[l], w_kv_mem[l].astype(BF16), bm=_pick(n_mem_rows, 512), bn=1024)
        kv = kv.reshape(-1, MEM_TOKENS, 2 * MEM_WIDTH)
        x = cross_attention(x, meta, cross_norm[l], w_q_mem[l].astype(BF16), kv, w_o_mem[l].astype(BF16),
                            bm=ATT_BLOCK)

        xnt, r2, bz, cnt, a = peer_router(x, peer_norm[l], w_q_peer[l].T.astype(BF16),
                                          peer_k1[l].astype(BF16), peer_k2[l].astype(BF16), bm=bm_mid)
        x = peer_experts(xnt, expert_u[l].astype(BF16), expert_v[l].T.astype(BF16), r2, bz, cnt, a, x,
                         bm=bm_mid)

    outs = []
    start = 0
    for (b, s) in groups:
        rows = b * s
        y = final_norm_slice(x, final_norm, start, rows, bm=_pick(int(np.gcd(rows, start)), 1024))
        outs.append(y.reshape(b, s, D_MODEL))
        start += rows
    return tuple(outs)


def kernel(x_prompt, x_sample, mem_prompt, mem_sample, mix_norm, w_in, attn_sink, gm_ln_g, gm_ln_b, gm_w_s, gm_b_s, w_att_proj, w_gm_proj, w_out, cross_norm, mem_norm, w_q_mem, w_kv_mem, w_o_mem, peer_norm, w_q_peer, peer_k1, peer_k2, expert_u, expert_v, final_norm):
    return _trunk((x_prompt, x_sample), (mem_prompt, mem_sample), mix_norm, w_in, attn_sink, gm_ln_g,
                  gm_ln_b, gm_w_s, gm_b_s, w_att_proj, w_gm_proj, w_out, cross_norm, mem_norm, w_q_mem,
                  w_kv_mem, w_o_mem, peer_norm, w_q_peer, peer_k1, peer_k2, expert_u, expert_v, final_norm)
```

```python
import numpy as np
import jax
import jax.numpy as jnp
from jax import lax
from jax.experimental import pallas as pl
from jax.experimental.pallas import tpu as pltpu

F32 = jnp.float32
BF16 = jnp.bfloat16

D_MODEL = 2048
HEAD_DIM = 128
N_Q_HEADS = 16
N_KV_HEADS = 4
Q_PER_KV = N_Q_HEADS // N_KV_HEADS
ATT_WIDTH = N_Q_HEADS * HEAD_DIM
KV_WIDTH = N_KV_HEADS * HEAD_DIM
WINDOW = 128
ROPE_THETA = 10000.0
GM_WIDTH = 2048
GM_GROUPS = 16
GM_CHUNK = 128
MEM_TOKENS = 256
MEM_HEADS = 4
MEM_HEAD_DIM = 128
MEM_WIDTH = MEM_HEADS * MEM_HEAD_DIM
PEER_HEADS = 8
PEER_QDIM = 256
PEER_HALF = PEER_QDIM // 2
N_KEYS = 128
N_EXPERTS = N_KEYS * N_KEYS
PEER_TOPK = 16
NORM_EPS = 1e-6
NEG_INF = -1e30
LOG2_E = 1.4426950408889634

COL_Q, COL_GU, COL_GZ, COL_GA, COL_GB = 0, 1, 2, 3, 4
COL_K = (ATT_WIDTH + 4 * D_MODEL) // KV_WIDTH
COL_V = COL_K + 1

LANES = 128
SUBLANES = 8
ATT_BLOCK = 512
SINK_LANES = LANES
HEADS_PER_PASS = 2
VMEM_LIMIT = 56 * 1024 * 1024


def _cparams(*sem):
    return pltpu.CompilerParams(dimension_semantics=sem, vmem_limit_bytes=VMEM_LIMIT)


def _rms(x, g):
    return x * lax.rsqrt(jnp.mean(x * x, axis=-1, keepdims=True) + NORM_EPS) * g


def _dot_nt(a, b):
    return lax.dot_general(a, b, (((1,), (1,)), ((), ())), preferred_element_type=F32)


def _norm_matmul_kernel(x_ref, g_ref, w_ref, o_ref, xn_ref):
    @pl.when(pl.program_id(1) == 0)
    def _():
        xn_ref[...] = _rms(x_ref[...], g_ref[...]).astype(BF16)

    o_ref[...] = jnp.dot(xn_ref[...], w_ref[...], preferred_element_type=F32).astype(o_ref.dtype)


def norm_matmul(x, g, w, *, bm, bn):
    t, d = x.shape
    n = w.shape[1]
    assert t % bm == 0 and n % bn == 0
    return pl.pallas_call(
        _norm_matmul_kernel,
        grid=(t // bm, n // bn),
        in_specs=[
            pl.BlockSpec((bm, d), lambda i, j: (i, 0)),
            pl.BlockSpec((1, d), lambda i, j: (0, 0)),
            pl.BlockSpec((d, bn), lambda i, j: (0, j)),
        ],
        out_specs=pl.BlockSpec((bm, bn), lambda i, j: (i, j)),
        out_shape=jax.ShapeDtypeStruct((t, n), BF16),
        scratch_shapes=[pltpu.VMEM((bm, d), BF16)],
        compiler_params=_cparams("parallel", "arbitrary"),
        name="norm_matmul",
    )(x, g.reshape(1, d), w)


def _attn_kernel(meta_ref, q_ref, kc_ref, kp_ref, kn_ref, vc_ref, vp_ref, vn_ref,
                 csc_ref, csp_ref, csn_ref, sink_ref, o_ref, q_scr, k_scr, v_scr, o_scr):
    n = pl.program_id(0)
    nb = ATT_BLOCK // WINDOW
    band = 3 * WINDOW
    k_lo = jnp.where(meta_ref[1, n] == 1, WINDOW, 0)
    k_hi = jnp.where(meta_ref[2, n] == 1, 2 * WINDOW, band)

    def rope(x, cs):
        return x * cs[:, :HEAD_DIM] + pltpu.roll(x, HEAD_DIM // 2, 1) * cs[:, HEAD_DIM:]

    cs_c = csc_ref[...]
    for hq in range(N_Q_HEADS):
        sl = slice(hq * HEAD_DIM, (hq + 1) * HEAD_DIM)
        q_scr[hq] = rope(q_ref[:, sl].astype(F32), cs_c).astype(BF16)
    cs_p = csp_ref[...]
    cs_n = csn_ref[...]
    for h in range(N_KV_HEADS):
        sl = slice(h * HEAD_DIM, (h + 1) * HEAD_DIM)
        k_scr[h, 0:WINDOW, :] = rope(kp_ref[:, sl].astype(F32), cs_p).astype(BF16)
        k_scr[h, WINDOW:WINDOW + ATT_BLOCK, :] = rope(kc_ref[:, sl].astype(F32), cs_c).astype(BF16)
        k_scr[h, WINDOW + ATT_BLOCK:, :] = rope(kn_ref[:, sl].astype(F32), cs_n).astype(BF16)
        v_scr[h, 0:WINDOW, :] = vp_ref[:, sl]
        v_scr[h, WINDOW:WINDOW + ATT_BLOCK, :] = vc_ref[:, sl]
        v_scr[h, WINDOW + ATT_BLOCK:, :] = vn_ref[:, sl]

    qi = lax.broadcasted_iota(jnp.int32, (WINDOW, band), 0)
    kc = lax.broadcasted_iota(jnp.int32, (WINDOW, band), 1)
    rel = jnp.abs(qi + WINDOW - kc) <= WINDOW
    scale = HEAD_DIM ** -0.5 * LOG2_E

    def band_step(r, carry):
        r0 = pl.multiple_of(r * WINDOW, WINDOW)
        lo = jnp.where(r == 0, k_lo, 0)
        hi = jnp.where(r == nb - 1, k_hi, band)
        bias = jnp.where(jnp.logical_and(rel, jnp.logical_and(kc >= lo, kc < hi)), 0.0, NEG_INF)
        for h0 in range(0, N_KV_HEADS, HEADS_PER_PASS):
            heads = range(h0, h0 + HEADS_PER_PASS)
            kbs = {h: k_scr[h, pl.ds(r0, band), :] for h in heads}
            vbs = {h: v_scr[h, pl.ds(r0, band), :] for h in heads}
            scores = {(h, g): _dot_nt(q_scr[Q_PER_KV * h + g, pl.ds(r0, WINDOW), :], kbs[h])
                      for h in heads for g in range(Q_PER_KV)}
            for h in heads:
                for g in range(Q_PER_KV):
                    hq = Q_PER_KV * h + g
                    s = scores[(h, g)] * scale + bias
                    sk = sink_ref[hq]
                    m = jnp.maximum(jnp.max(s, axis=-1, keepdims=True), sk)
                    e = jnp.concatenate([jnp.exp2(s[:, j * WINDOW:(j + 1) * WINDOW] - m) for j in range(3)],
                                        axis=-1)
                    den = jnp.sum(e, axis=-1, keepdims=True) + jnp.exp2(sk - m)
                    o = jnp.dot(e.astype(BF16), vbs[h], preferred_element_type=F32) * (1.0 / den)
                    o_scr[hq, pl.ds(r0, WINDOW), :] = o.astype(BF16)
        return carry

    lax.fori_loop(0, nb, band_step, 0)
    for hq in range(N_Q_HEADS):
        o_ref[:, hq * HEAD_DIM:(hq + 1) * HEAD_DIM] = o_scr[hq]


def banded_attention(proj, meta, cs_table, sink_col):
    t = proj.shape[0]
    nblk = t // ATT_BLOCK
    sub = ATT_BLOCK // WINDOW
    n_small = t // WINDOW
    n_pos_small = cs_table.shape[0] // WINDOW
    grid_spec = pltpu.PrefetchScalarGridSpec(
        num_scalar_prefetch=1,
        grid=(nblk,),
        in_specs=[
            pl.BlockSpec((ATT_BLOCK, ATT_WIDTH), lambda n, m: (n, COL_Q)),
            pl.BlockSpec((ATT_BLOCK, KV_WIDTH), lambda n, m: (n, COL_K)),
            pl.BlockSpec((WINDOW, KV_WIDTH), lambda n, m: (jnp.maximum(n * sub - 1, 0), COL_K)),
            pl.BlockSpec((WINDOW, KV_WIDTH), lambda n, m: (jnp.minimum(n * sub + sub, n_small - 1), COL_K)),
            pl.BlockSpec((ATT_BLOCK, KV_WIDTH), lambda n, m: (n, COL_V)),
            pl.BlockSpec((WINDOW, KV_WIDTH), lambda n, m: (jnp.maximum(n * sub - 1, 0), COL_V)),
            pl.BlockSpec((WINDOW, KV_WIDTH), lambda n, m: (jnp.minimum(n * sub + sub, n_small - 1), COL_V)),
            pl.BlockSpec((ATT_BLOCK, 2 * HEAD_DIM), lambda n, m: (m[0, n], 0)),
            pl.BlockSpec((WINDOW, 2 * HEAD_DIM), lambda n, m: (jnp.maximum(m[0, n] * sub - 1, 0), 0)),
            pl.BlockSpec((WINDOW, 2 * HEAD_DIM),
                         lambda n, m: (jnp.minimum(m[0, n] * sub + sub, n_pos_small - 1), 0)),
            pl.BlockSpec((N_Q_HEADS, WINDOW, SINK_LANES), lambda n, m: (0, 0, 0)),
        ],
        out_specs=pl.BlockSpec((ATT_BLOCK, ATT_WIDTH), lambda n, m: (n, 0)),
        scratch_shapes=[
            pltpu.VMEM((N_Q_HEADS, ATT_BLOCK, HEAD_DIM), BF16),
            pltpu.VMEM((N_KV_HEADS, ATT_BLOCK + 2 * WINDOW, HEAD_DIM), BF16),
            pltpu.VMEM((N_KV_HEADS, ATT_BLOCK + 2 * WINDOW, HEAD_DIM), BF16),
            pltpu.VMEM((N_Q_HEADS, ATT_BLOCK, HEAD_DIM), BF16),
        ],
    )
    return pl.pallas_call(
        _attn_kernel,
        grid_spec=grid_spec,
        out_shape=jax.ShapeDtypeStruct((t, ATT_WIDTH), BF16),
        compiler_params=_cparams("parallel"),
        name="banded_attention",
    )(meta, proj, proj, proj, proj, proj, proj, proj, cs_table, cs_table, cs_table, sink_col)


def _gmlp_kernel(gu_ref, gz_ref, lng_ref, lnb_ref, ws_ref, bs_ref, o_ref, zn_scr, u_scr):
    z = jax.nn.gelu(gz_ref[...].astype(F32))
    mu = jnp.mean(z, axis=-1, keepdims=True)
    zc = z - mu
    var = jnp.mean(zc * zc, axis=-1, keepdims=True)
    zn_scr[...] = (zc * lax.rsqrt(var + NORM_EPS) * lng_ref[...] + lnb_ref[...]).astype(BF16)
    u_scr[...] = jax.nn.gelu(gu_ref[...])
    gd = GM_WIDTH // GM_GROUPS
    for c in range(o_ref.shape[0] // GM_CHUNK):
        rs = slice(c * GM_CHUNK, (c + 1) * GM_CHUNK)
        for g in range(GM_GROUPS):
            cs = slice(g * gd, (g + 1) * gd)
            mixed = jnp.dot(ws_ref[g], zn_scr[rs, cs], preferred_element_type=F32) + bs_ref[g]
            o_ref[rs, cs] = (u_scr[rs, cs].astype(F32) * mixed).astype(BF16)


def spatial_gating(proj, ln_g, ln_b, w_s, b_s_b, *, bm):
    t = proj.shape[0]
    return pl.pallas_call(
        _gmlp_kernel,
        grid=(t // bm,),
        in_specs=[
            pl.BlockSpec((bm, GM_WIDTH), lambda i: (i, COL_GU)),
            pl.BlockSpec((bm, GM_WIDTH), lambda i: (i, COL_GZ)),
            pl.BlockSpec((1, GM_WIDTH), lambda i: (0, 0)),
            pl.BlockSpec((1, GM_WIDTH), lambda i: (0, 0)),
            pl.BlockSpec((GM_GROUPS, GM_CHUNK, GM_CHUNK), lambda i: (0, 0, 0)),
            pl.BlockSpec((GM_GROUPS, GM_CHUNK, GM_WIDTH // GM_GROUPS), lambda i: (0, 0, 0)),
        ],
        out_specs=pl.BlockSpec((bm, GM_WIDTH), lambda i: (i, 0)),
        out_shape=jax.ShapeDtypeStruct((t, GM_WIDTH), BF16),
        scratch_shapes=[pltpu.VMEM((bm, GM_WIDTH), BF16), pltpu.VMEM((bm, GM_WIDTH), BF16)],
        compiler_params=_cparams("parallel"),
        name="spatial_gating",
    )(proj, proj, ln_g.reshape(1, -1), ln_b.reshape(1, -1), w_s, b_s_b)


def _sigmoid(x):
    return 1.0 / (1.0 + jnp.exp(-x))


def _merge_kernel(att_ref, gm_ref, ga_ref, gb_ref, x_ref, wa_ref, wg_ref, wo_ref, y_ref):
    a = jnp.dot(att_ref[...], wa_ref[...], preferred_element_type=F32)
    b = jnp.dot(gm_ref[...], wg_ref[...], preferred_element_type=F32)
    merged = _sigmoid(ga_ref[...].astype(F32)) * a + _sigmoid(gb_ref[...].astype(F32)) * b
    y_ref[...] = x_ref[...] + jnp.dot(merged.astype(BF16), wo_ref[...], preferred_element_type=F32)


def merge_project(att, gm, proj, x, wa, wg, wo, *, bm):
    t, d = x.shape
    const = dict(pipeline_mode=pl.Buffered(1))
    return pl.pallas_call(
        _merge_kernel,
        grid=(t // bm,),
        in_specs=[
            pl.BlockSpec((bm, ATT_WIDTH), lambda i: (i, 0)),
            pl.BlockSpec((bm, GM_WIDTH), lambda i: (i, 0)),
            pl.BlockSpec((bm, d), lambda i: (i, COL_GA)),
            pl.BlockSpec((bm, d), lambda i: (i, COL_GB)),
            pl.BlockSpec((bm, d), lambda i: (i, 0)),
            pl.BlockSpec((ATT_WIDTH, d), lambda i: (0, 0), **const),
            pl.BlockSpec((GM_WIDTH, d), lambda i: (0, 0), **const),
            pl.BlockSpec((d, d), lambda i: (0, 0), **const),
        ],
        out_specs=pl.BlockSpec((bm, d), lambda i: (i, 0)),
        out_shape=jax.ShapeDtypeStruct((t, d), F32),
        compiler_params=_cparams("parallel"),
        name="merge_project",
    )(att, gm, proj, proj, x, wa, wg, wo)


def _cross_kernel(meta_ref, x_ref, g_ref, wq_ref, kv_ref, wo_ref, y_ref):
    x = x_ref[...]
    xn = _rms(x, g_ref[...]).astype(BF16)
    q = jnp.dot(xn, wq_ref[...], preferred_element_type=F32).astype(BF16)
    scale = MEM_HEAD_DIM ** -0.5
    outs = []
    for h in range(MEM_HEADS):
        sl = slice(h * MEM_HEAD_DIM, (h + 1) * MEM_HEAD_DIM)
        kh = kv_ref[0, :, sl]
        vh = kv_ref[0, :, MEM_WIDTH + h * MEM_HEAD_DIM:MEM_WIDTH + (h + 1) * MEM_HEAD_DIM]
        s = _dot_nt(q[:, sl], kh) * scale
        m = jnp.max(s, axis=-1, keepdims=True)
        e = jnp.exp(s - m)
        p = (e / jnp.sum(e, axis=-1, keepdims=True)).astype(BF16)
        outs.append(jnp.dot(p, vh, preferred_element_type=F32).astype(BF16))
    o = jnp.concatenate(outs, axis=-1)
    y_ref[...] = x + jnp.dot(o, wo_ref[...], preferred_element_type=F32)


def cross_attention(x, meta, g, wq, kv, wo, *, bm):
    t, d = x.shape
    grid_spec = pltpu.PrefetchScalarGridSpec(
        num_scalar_prefetch=1,
        grid=(t // bm,),
        in_specs=[
            pl.BlockSpec((bm, d), lambda i, m: (i, 0)),
            pl.BlockSpec((1, d), lambda i, m: (0, 0)),
            pl.BlockSpec((d, MEM_WIDTH), lambda i, m: (0, 0)),
            pl.BlockSpec((1, MEM_TOKENS, 2 * MEM_WIDTH), lambda i, m: (m[3, i], 0, 0)),
            pl.BlockSpec((MEM_WIDTH, d), lambda i, m: (0, 0)),
        ],
        out_specs=pl.BlockSpec((bm, d), lambda i, m: (i, 0)),
    )
    return pl.pallas_call(
        _cross_kernel,
        grid_spec=grid_spec,
        out_shape=jax.ShapeDtypeStruct((t, d), F32),
        compiler_params=_cparams("parallel"),
        name="cross_attention",
    )(meta, x, g.reshape(1, d), wq, kv, wo)


N_TOP = PEER_TOPK + 1
CAND = [(a, b) for a in range(N_TOP) for b in range(N_TOP // (a + 1))]
CAND_ROWS = 8 * SUBLANES
assert len(CAND) <= CAND_ROWS
XPOSE_COLS = 256


def _oddeven_merge_sort(n):
    def merge(lo, hi, r):
        step = r * 2
        if step < hi - lo:
            yield from merge(lo, hi, step)
            yield from merge(lo + r, hi, step)
            yield from [(i, i + r) for i in range(lo + r, hi - r, step)]
        else:
            yield (lo, lo + r)

    def sort(lo, hi):
        if hi - lo >= 1:
            mid = lo + (hi - lo) // 2
            yield from sort(lo, mid)
            yield from sort(mid + 1, hi)
            yield from merge(lo, hi, 1)

    return list(sort(0, n - 1))


def _top_values(s, n_top, one_at_a_time=False):
    nblk = s.shape[0] // SUBLANES
    v = [s[d * SUBLANES:(d + 1) * SUBLANES, :] for d in range(nblk)]
    for a, b in _oddeven_merge_sort(nblk):
        v[a], v[b] = jnp.maximum(v[a], v[b]), jnp.minimum(v[a], v[b])
    sub = lax.broadcasted_iota(jnp.int32, v[0].shape, 0)
    vals = []
    for k in range(n_top):
        m = jnp.max(v[0], axis=0, keepdims=True)
        vals.append(m)
        remaining = n_top - 1 - k
        if remaining == 0:
            break
        hit = v[0] == m
        if one_at_a_time:
            first = jnp.min(jnp.where(hit, sub, SUBLANES), axis=0, keepdims=True)
            hit = sub == first
        for d in range(min(nblk, remaining)):
            v[d] = jnp.where(hit, v[d + 1] if d + 1 < nblk else NEG_INF, v[d])
    return vals


def _count_above(vals, x, strict):
    assert len(vals) == 16

    def above(v):
        return (v > x) if strict else (v >= x)

    bits = []

    def pivot(level, lo, size, depth=0):
        mid = lo + size // 2
        if depth == level:
            return vals[mid]
        return jnp.where(bits[depth], pivot(level, mid + 1, size // 2, depth + 1),
                         pivot(level, lo, size // 2, depth + 1))

    count = jnp.where(above(vals[15]), 1.0, 0.0)
    for level in range(4):
        bits.append(above(pivot(level, 0, 15)))
        count = count + jnp.where(bits[level], float(8 >> level), 0.0)
    return count


def _router_kernel(x_ref, g_ref, wqt_ref, k1_ref, k2_ref, xnt_ref, r2_ref, bz_ref, cnt_ref, a_ref,
                   qt_scr, c_scr):
    d = x_ref.shape[1]
    x = x_ref[...]
    inv = lax.rsqrt(jnp.mean(x * x, axis=-1, keepdims=True) + NORM_EPS)
    for c0 in range(0, d, XPOSE_COLS):
        cols = slice(c0, c0 + XPOSE_COLS)
        xn_c = x_ref[:, cols] * inv * g_ref[:, cols]
        xnt_ref[cols, :] = xn_c.T.astype(BF16)
    qt_scr[...] = jnp.dot(wqt_ref[...], xnt_ref[...],
                          preferred_element_type=F32).astype(BF16)
    c_scr[...] = jnp.full(c_scr.shape, NEG_INF, F32)

    def head(h, carry):
        off = pl.multiple_of(h * PEER_QDIM, PEER_QDIM)
        s1 = jnp.dot(k1_ref[...], qt_scr[pl.ds(off, PEER_HALF), :], preferred_element_type=F32)
        s2 = jnp.dot(k2_ref[...], qt_scr[pl.ds(off + PEER_HALF, PEER_HALF), :],
                     preferred_element_type=F32)
        v1 = _top_values(s1, N_TOP)
        v2 = _top_values(s2, N_TOP)
        for r, (a, b) in enumerate(CAND):
            c_scr[r:r + 1, :] = v1[a] + v2[b]
        c0 = c_scr[...]
        csort = _top_values(c0, N_TOP, one_at_a_time=True)
        thr = 0.5 * (csort[PEER_TOPK - 1] + csort[PEER_TOPK])
        top = v1[0] + v2[0]
        z = jnp.sum(jnp.where(c0 >= thr, jnp.exp(c0 - top), 0.0), axis=0, keepdims=True)
        r2_ref[h] = _count_above(v2[:PEER_TOPK], s2, strict=True).astype(BF16)
        bz_ref[h] = (jnp.exp(s2 - v2[0]) / z).astype(BF16)
        cnt_ref[h] = _count_above(v2[:PEER_TOPK], thr - s1, strict=False)
        a_ref[h] = jnp.exp(s1 - v1[0])
        return carry

    lax.fori_loop(0, PEER_HEADS, head, 0)


def peer_router(x, g, wqt, k1, k2, *, bm):
    t, d = x.shape
    head_spec = pl.BlockSpec((PEER_HEADS, N_KEYS, bm), lambda i: (0, 0, i))
    return pl.pallas_call(
        _router_kernel,
        grid=(t // bm,),
        in_specs=[
            pl.BlockSpec((bm, d), lambda i: (i, 0)),
            pl.BlockSpec((1, d), lambda i: (0, 0)),
            pl.BlockSpec((PEER_HEADS * PEER_QDIM, d), lambda i: (0, 0)),
            pl.BlockSpec((N_KEYS, PEER_HALF), lambda i: (0, 0)),
            pl.BlockSpec((N_KEYS, PEER_HALF), lambda i: (0, 0)),
        ],
        out_specs=[pl.BlockSpec((d, bm), lambda i: (0, i)), head_spec, head_spec, head_spec, head_spec],
        out_shape=[jax.ShapeDtypeStruct((d, t), BF16),
                   jax.ShapeDtypeStruct((PEER_HEADS, N_KEYS, t), BF16),
                   jax.ShapeDtypeStruct((PEER_HEADS, N_KEYS, t), BF16),
                   jax.ShapeDtypeStruct((PEER_HEADS, N_KEYS, t), F32),
                   jax.ShapeDtypeStruct((PEER_HEADS, N_KEYS, t), F32)],
        scratch_shapes=[pltpu.VMEM((PEER_HEADS * PEER_QDIM, bm), BF16), pltpu.VMEM((CAND_ROWS, bm), F32)],
        compiler_params=_cparams("parallel"),
        name="peer_router",
    )(x, g.reshape(1, d), wqt, k1, k2)


EXPERT_TILE = 1024
EXPERT_CHUNK = 256
KEYS_PER_TILE = EXPERT_TILE // N_KEYS
KEYS_PER_CHUNK = EXPERT_CHUNK // N_KEYS
CHUNKS_PER_TILE = EXPERT_TILE // EXPERT_CHUNK
TOKEN_GROUP = 2 * LANES
assert KEYS_PER_TILE == SUBLANES


def _expert_kernel(xnt_ref, u_ref, vt_ref, r2_ref, bz_ref, cnt_ref, a_ref, x_ref, y_ref, p_scr, acc_scr):
    e = pl.program_id(1)
    bm = x_ref.shape[0]

    @pl.when(e == 0)
    def _():
        acc_scr[...] = jnp.zeros(acc_scr.shape, F32)

    for c in range(CHUNKS_PER_TILE):
        crow = slice(c * EXPERT_CHUNK, (c + 1) * EXPERT_CHUNK)
        h = jnp.dot(u_ref[crow, :], xnt_ref[...], preferred_element_type=F32)
        for kk in range(KEYS_PER_CHUNK):
            il = c * KEYS_PER_CHUNK + kk
            rows = slice(il * N_KEYS, (il + 1) * N_KEYS)
            for tg in range(bm // TOKEN_GROUP):
                ts = slice(tg * TOKEN_GROUP, (tg + 1) * TOKEN_GROUP)
                g = jax.nn.gelu(h[kk * N_KEYS:(kk + 1) * N_KEYS, ts].astype(BF16))
                p = jnp.zeros((N_KEYS, TOKEN_GROUP), BF16)
                for hd in range(PEER_HEADS):
                    cb = jnp.broadcast_to(cnt_ref[hd, il:il + 1, ts], (N_KEYS, TOKEN_GROUP)).astype(BF16)
                    ab = jnp.broadcast_to(a_ref[hd, il:il + 1, ts], (N_KEYS, TOKEN_GROUP)).astype(BF16)
                    p = p + jnp.where(r2_ref[hd, :, ts] < cb, bz_ref[hd, :, ts], 0.0) * ab
                p_scr[rows, ts] = p * g
    acc_scr[...] += jnp.dot(vt_ref[...], p_scr[...], preferred_element_type=F32)

    @pl.when(e == pl.num_programs(1) - 1)
    def _():
        for c0 in range(0, y_ref.shape[1], XPOSE_COLS):
            cols = slice(c0, c0 + XPOSE_COLS)
            y_ref[:, cols] = x_ref[:, cols] + acc_scr[cols, :].T


def peer_experts(xnt, u, vt, r2, bz, cnt, a, x, *, bm):
    t, d = x.shape
    assert bm % TOKEN_GROUP == 0
    head_full = pl.BlockSpec((PEER_HEADS, N_KEYS, bm), lambda i, e: (0, 0, i))
    head_rows = pl.BlockSpec((PEER_HEADS, KEYS_PER_TILE, bm), lambda i, e: (0, e, i))
    return pl.pallas_call(
        _expert_kernel,
        grid=(t // bm, u.shape[0] // EXPERT_TILE),
        in_specs=[
            pl.BlockSpec((d, bm), lambda i, e: (0, i)),
            pl.BlockSpec((EXPERT_TILE, d), lambda i, e: (e, 0)),
            pl.BlockSpec((d, EXPERT_TILE), lambda i, e: (0, e)),
            head_full, head_full, head_rows, head_rows,
            pl.BlockSpec((bm, d), lambda i, e: (i, 0)),
        ],
        out_specs=pl.BlockSpec((bm, d), lambda i, e: (i, 0)),
        out_shape=jax.ShapeDtypeStruct((t, d), F32),
        scratch_shapes=[
            pltpu.VMEM((EXPERT_TILE, bm), BF16),
            pltpu.VMEM((d, bm), F32),
        ],
        compiler_params=_cparams("parallel", "arbitrary"),
        name="peer_experts",
    )(xnt, u, vt, r2, bz, cnt, a, x)


def _final_norm_kernel(x_ref, g_ref, y_ref):
    y_ref[...] = _rms(x_ref[...], g_ref[...])


def final_norm_slice(x, g, start, rows, *, bm):
    d = x.shape[1]
    off = start // bm
    return pl.pallas_call(
        _final_norm_kernel,
        grid=(rows // bm,),
        in_specs=[pl.BlockSpec((bm, d), lambda i: (i + off, 0)), pl.BlockSpec((1, d), lambda i: (0, 0))],
        out_specs=pl.BlockSpec((bm, d), lambda i: (i, 0)),
        out_shape=jax.ShapeDtypeStruct((rows, d), F32),
        compiler_params=_cparams("parallel"),
        name="final_norm",
    )(x, g.reshape(1, d))


def _block_meta(groups, blk):
    pos, first, last, batch = [], [], [], []
    b0 = 0
    for (b, s) in groups:
        per = s // blk
        for bi in range(b):
            for j in range(per):
                pos.append(j)
                first.append(int(j == 0))
                last.append(int(j == per - 1))
                batch.append(b0 + bi)
        b0 += b
    return jnp.asarray(np.array([pos, first, last, batch], dtype=np.int32))


def _rope_table(max_pos):
    half = HEAD_DIM // 2
    inv = ROPE_THETA ** (-jnp.arange(half, dtype=F32) / half)
    ang = jnp.arange(max_pos, dtype=F32)[:, None] * inv[None, :]
    cos, sin = jnp.cos(ang), jnp.sin(ang)
    return jnp.concatenate([cos, cos, -sin, sin], axis=-1)


def _pick(t, pref):
    while t % pref:
        pref //= 2
    return pref


def _trunk(xs, mems, mix_norm, w_in, attn_sink, gm_ln_g, gm_ln_b, gm_w_s, gm_b_s, w_att_proj, w_gm_proj,
           w_out, cross_norm, mem_norm, w_q_mem, w_kv_mem, w_o_mem, peer_norm, w_q_peer, peer_k1, peer_k2,
           expert_u, expert_v, final_norm):
    groups = [(x.shape[0], x.shape[1]) for x in xs]
    depth = w_in.shape[0]
    for (_, s) in groups:
        assert s % ATT_BLOCK == 0
    x = jnp.concatenate([x.reshape(-1, D_MODEL) for x in xs], axis=0)
    mem = jnp.concatenate([m.reshape(-1, D_MODEL) for m in mems], axis=0)
    t = x.shape[0]
    n_mem_rows = mem.shape[0]
    meta = _block_meta(groups, ATT_BLOCK)
    cs_table = _rope_table(max(s for _, s in groups))

    bm_big = _pick(t, 1024)
    bm_mid = _pick(t, 512)
    bm_small = _pick(t, 256)
    in_splits = np.cumsum([ATT_WIDTH, KV_WIDTH, KV_WIDTH, GM_WIDTH, GM_WIDTH, D_MODEL]).tolist()

    for l in range(depth):
        q_w, k_w, v_w, gu_w, gz_w, ga_w, gb_w = jnp.split(w_in[l], in_splits, axis=1)
        w_in_p = jnp.concatenate([q_w, gu_w, gz_w, ga_w, gb_w, k_w, v_w], axis=1).astype(BF16)
        sink_col = jnp.broadcast_to((attn_sink[l].astype(F32) * LOG2_E)[:, None, None],
                                    (N_Q_HEADS, WINDOW, SINK_LANES))
        b_s_b = jnp.broadcast_to(gm_b_s[l].astype(F32)[:, :, None],
                                 (GM_GROUPS, GM_CHUNK, GM_WIDTH // GM_GROUPS))

        proj = norm_matmul(x, mix_norm[l], w_in_p, bm=bm_big, bn=1024)
        att = banded_attention(proj, meta, cs_table, sink_col)
        gm = spatial_gating(proj, gm_ln_g[l], gm_ln_b[l], gm_w_s[l].astype(BF16), b_s_b, bm=ATT_BLOCK)
        x = merge_project(att, gm, proj, x, w_att_proj[l].astype(BF16), w_gm_proj[l].astype(BF16),
                          w_out[l].astype(BF16), bm=bm_small)

        kv = norm_matmul(mem, mem_norm[l], w_kv_mem[l].astype(BF16), bm=_pick(n_mem_rows, 512), bn=1024)
        kv = kv.reshape(-1, MEM_TOKENS, 2 * MEM_WIDTH)
        x = cross_attention(x, meta, cross_norm[l], w_q_mem[l].astype(BF16), kv, w_o_mem[l].astype(BF16),
                            bm=ATT_BLOCK)

        xnt, r2, bz, cnt, a = peer_router(x, peer_norm[l], w_q_peer[l].T.astype(BF16),
                                          peer_k1[l].astype(BF16), peer_k2[l].astype(BF16), bm=bm_mid)
        x = peer_experts(xnt, expert_u[l].astype(BF16), expert_v[l].T.astype(BF16), r2, bz, cnt, a, x,
                         bm=bm_mid)

    outs = []
    start = 0
    for (b, s) in groups:
        rows = b * s
        y = final_norm_slice(x, final_norm, start, rows, bm=_pick(int(np.gcd(rows, start)), 1024))
        outs.append(y.reshape(b, s, D_MODEL))
        start += rows
    return tuple(outs)


def kernel(x_prompt, x_sample, mem_prompt, mem_sample, mix_norm, w_in, attn_sink, gm_ln_g, gm_ln_b, gm_w_s, gm_b_s, w_att_proj, w_gm_proj, w_out, cross_norm, mem_norm, w_q_mem, w_kv_mem, w_o_mem, peer_norm, w_q_peer, peer_k1, peer_k2, expert_u, expert_v, final_norm):
    return _trunk((x_prompt, x_sample), (mem_prompt, mem_sample), mix_norm, w_in, attn_sink, gm_ln_g,
                  gm_ln_b, gm_w_s, gm_b_s, w_att_proj, w_gm_proj, w_out, cross_norm, mem_norm, w_q_mem,
                  w_kv_mem, w_o_mem, peer_norm, w_q_peer, peer_k1, peer_k2, expert_u, expert_v, final_norm)
```

```python
import numpy as np
import jax
import jax.numpy as jnp
from jax import lax
from jax.experimental import pallas as pl
from jax.experimental.pallas import tpu as pltpu

F32 = jnp.float32
BF16 = jnp.bfloat16

D_MODEL = 2048
HEAD_DIM = 128
N_Q_HEADS = 16
N_KV_HEADS = 4
Q_PER_KV = N_Q_HEADS // N_KV_HEADS
ATT_WIDTH = N_Q_HEADS * HEAD_DIM
KV_WIDTH = N_KV_HEADS * HEAD_DIM
WINDOW = 128
ROPE_THETA = 10000.0
GM_WIDTH = 2048
GM_GROUPS = 16
GM_CHUNK = 128
MEM_TOKENS = 256
MEM_HEADS = 4
MEM_HEAD_DIM = 128
MEM_WIDTH = MEM_HEADS * MEM_HEAD_DIM
PEER_HEADS = 8
PEER_QDIM = 256
PEER_HALF = PEER_QDIM // 2
N_KEYS = 128
N_EXPERTS = N_KEYS * N_KEYS
PEER_TOPK = 16
NORM_EPS = 1e-6
NEG_INF = -1e30
LOG2_E = 1.4426950408889634

COL_Q, COL_GU, COL_GZ, COL_GA, COL_GB = 0, 1, 2, 3, 4
COL_K = (ATT_WIDTH + 4 * D_MODEL) // KV_WIDTH
COL_V = COL_K + 1

LANES = 128
SUBLANES = 8
ATT_BLOCK = 512
SINK_LANES = LANES
HEADS_PER_PASS = 2
VMEM_LIMIT = 56 * 1024 * 1024


def _cparams(*sem):
    return pltpu.CompilerParams(dimension_semantics=sem, vmem_limit_bytes=VMEM_LIMIT)


def _rms(x, g):
    return x * lax.rsqrt(jnp.mean(x * x, axis=-1, keepdims=True) + NORM_EPS) * g


def _dot_nt(a, b):
    return lax.dot_general(a, b, (((1,), (1,)), ((), ())), preferred_element_type=F32)


def _norm_matmul_kernel(x_ref, g_ref, w_ref, o_ref, xn_ref):
    @pl.when(pl.program_id(1) == 0)
    def _():
        xn_ref[...] = _rms(x_ref[...], g_ref[...]).astype(BF16)

    o_ref[...] = jnp.dot(xn_ref[...], w_ref[...], preferred_element_type=F32).astype(o_ref.dtype)


def norm_matmul(x, g, w, *, bm, bn):
    t, d = x.shape
    n = w.shape[1]
    assert t % bm == 0 and n % bn == 0
    return pl.pallas_call(
        _norm_matmul_kernel,
        grid=(t // bm, n // bn),
        in_specs=[
            pl.BlockSpec((bm, d), lambda i, j: (i, 0)),
            pl.BlockSpec((1, d), lambda i, j: (0, 0)),
            pl.BlockSpec((d, bn), lambda i, j: (0, j)),
        ],
        out_specs=pl.BlockSpec((bm, bn), lambda i, j: (i, j)),
        out_shape=jax.ShapeDtypeStruct((t, n), BF16),
        scratch_shapes=[pltpu.VMEM((bm, d), BF16)],
        compiler_params=_cparams("parallel", "arbitrary"),
        name="norm_matmul",
    )(x, g.reshape(1, d), w)


def _attn_kernel(meta_ref, q_ref, kc_ref, kp_ref, kn_ref, vc_ref, vp_ref, vn_ref,
                 csc_ref, csp_ref, csn_ref, sink_ref, o_ref, q_scr, k_scr, v_scr, o_scr):
    n = pl.program_id(0)
    nb = ATT_BLOCK // WINDOW
    band = 3 * WINDOW
    k_lo = jnp.where(meta_ref[1, n] == 1, WINDOW, 0)
    k_hi = jnp.where(meta_ref[2, n] == 1, 2 * WINDOW, band)

    def rope(x, cs):
        return x * cs[:, :HEAD_DIM] + pltpu.roll(x, HEAD_DIM // 2, 1) * cs[:, HEAD_DIM:]

    cs_c = csc_ref[...]
    for hq in range(N_Q_HEADS):
        sl = slice(hq * HEAD_DIM, (hq + 1) * HEAD_DIM)
        q_scr[hq] = rope(q_ref[:, sl].astype(F32), cs_c).astype(BF16)
    cs_p = csp_ref[...]
    cs_n = csn_ref[...]
    for h in range(N_KV_HEADS):
        sl = slice(h * HEAD_DIM, (h + 1) * HEAD_DIM)
        k_scr[h, 0:WINDOW, :] = rope(kp_ref[:, sl].astype(F32), cs_p).astype(BF16)
        k_scr[h, WINDOW:WINDOW + ATT_BLOCK, :] = rope(kc_ref[:, sl].astype(F32), cs_c).astype(BF16)
        k_scr[h, WINDOW + ATT_BLOCK:, :] = rope(kn_ref[:, sl].astype(F32), cs_n).astype(BF16)
        v_scr[h, 0:WINDOW, :] = vp_ref[:, sl]
        v_scr[h, WINDOW:WINDOW + ATT_BLOCK, :] = vc_ref[:, sl]
        v_scr[h, WINDOW + ATT_BLOCK:, :] = vn_ref[:, sl]

    qi = lax.broadcasted_iota(jnp.int32, (WINDOW, band), 0)
    kc = lax.broadcasted_iota(jnp.int32, (WINDOW, band), 1)
    rel = jnp.abs(qi + WINDOW - kc) <= WINDOW
    scale = HEAD_DIM ** -0.5 * LOG2_E

    def band_step(r, carry):
        r0 = pl.multiple_of(r * WINDOW, WINDOW)
        lo = jnp.where(r == 0, k_lo, 0)
        hi = jnp.where(r == nb - 1, k_hi, band)
        bias = jnp.where(jnp.logical_and(rel, jnp.logical_and(kc >= lo, kc < hi)), 0.0, NEG_INF)
        for h0 in range(0, N_KV_HEADS, HEADS_PER_PASS):
            heads = range(h0, h0 + HEADS_PER_PASS)
            kbs = {h: k_scr[h, pl.ds(r0, band), :] for h in heads}
            vbs = {h: v_scr[h, pl.ds(r0, band), :] for h in heads}
            scores = {(h, g): _dot_nt(q_scr[Q_PER_KV * h + g, pl.ds(r0, WINDOW), :], kbs[h])
                      for h in heads for g in range(Q_PER_KV)}
            for h in heads:
                for g in range(Q_PER_KV):
                    hq = Q_PER_KV * h + g
                    s = scores[(h, g)] * scale + bias
                    sk = sink_ref[hq]
                    m = jnp.maximum(jnp.max(s, axis=-1, keepdims=True), sk)
                    e = jnp.concatenate([jnp.exp2(s[:, j * WINDOW:(j + 1) * WINDOW] - m) for j in range(3)],
                                        axis=-1)
                    den = jnp.sum(e, axis=-1, keepdims=True) + jnp.exp2(sk - m)
                    o = jnp.dot(e.astype(BF16), vbs[h], preferred_element_type=F32) * (1.0 / den)
                    o_scr[hq, pl.ds(r0, WINDOW), :] = o.astype(BF16)
        return carry

    lax.fori_loop(0, nb, band_step, 0)
    for hq in range(N_Q_HEADS):
        o_ref[:, hq * HEAD_DIM:(hq + 1) * HEAD_DIM] = o_scr[hq]


def banded_attention(proj, meta, cs_table, sink_col):
    t = proj.shape[0]
    nblk = t // ATT_BLOCK
    sub = ATT_BLOCK // WINDOW
    n_small = t // WINDOW
    n_pos_small = cs_table.shape[0] // WINDOW
    grid_spec = pltpu.PrefetchScalarGridSpec(
        num_scalar_prefetch=1,
        grid=(nblk,),
        in_specs=[
            pl.BlockSpec((ATT_BLOCK, ATT_WIDTH), lambda n, m: (n, COL_Q)),
            pl.BlockSpec((ATT_BLOCK, KV_WIDTH), lambda n, m: (n, COL_K)),
            pl.BlockSpec((WINDOW, KV_WIDTH), lambda n, m: (jnp.maximum(n * sub - 1, 0), COL_K)),
            pl.BlockSpec((WINDOW, KV_WIDTH), lambda n, m: (jnp.minimum(n * sub + sub, n_small - 1), COL_K)),
            pl.BlockSpec((ATT_BLOCK, KV_WIDTH), lambda n, m: (n, COL_V)),
            pl.BlockSpec((WINDOW, KV_WIDTH), lambda n, m: (jnp.maximum(n * sub - 1, 0), COL_V)),
            pl.BlockSpec((WINDOW, KV_WIDTH), lambda n, m: (jnp.minimum(n * sub + sub, n_small - 1), COL_V)),
            pl.BlockSpec((ATT_BLOCK, 2 * HEAD_DIM), lambda n, m: (m[0, n], 0)),
            pl.BlockSpec((WINDOW, 2 * HEAD_DIM), lambda n, m: (jnp.maximum(m[0, n] * sub - 1, 0), 0)),
            pl.BlockSpec((WINDOW, 2 * HEAD_DIM),
                         lambda n, m: (jnp.minimum(m[0, n] * sub + sub, n_pos_small - 1), 0)),
            pl.BlockSpec((N_Q_HEADS, WINDOW, SINK_LANES), lambda n, m: (0, 0, 0)),
        ],
        out_specs=pl.BlockSpec((ATT_BLOCK, ATT_WIDTH), lambda n, m: (n, 0)),
        scratch_shapes=[
            pltpu.VMEM((N_Q_HEADS, ATT_BLOCK, HEAD_DIM), BF16),
            pltpu.VMEM((N_KV_HEADS, ATT_BLOCK + 2 * WINDOW, HEAD_DIM), BF16),
            pltpu.VMEM((N_KV_HEADS, ATT_BLOCK + 2 * WINDOW, HEAD_DIM), BF16),
            pltpu.VMEM((N_Q_HEADS, ATT_BLOCK, HEAD_DIM), BF16),
        ],
    )
    return pl.pallas_call(
        _attn_kernel,
        grid_spec=grid_spec,
        out_shape=jax.ShapeDtypeStruct((t, ATT_WIDTH), BF16),
        compiler_params=_cparams("parallel"),
        name="banded_attention",
    )(meta, proj, proj, proj, proj, proj, proj, proj, cs_table, cs_table, cs_table, sink_col)


def _gmlp_kernel(gu_ref, gz_ref, lng_ref, lnb_ref, ws_ref, bs_ref, o_ref, zn_scr, u_scr):
    z = jax.nn.gelu(gz_ref[...].astype(F32))
    mu = jnp.mean(z, axis=-1, keepdims=True)
    zc = z - mu
    var = jnp.mean(zc * zc, axis=-1, keepdims=True)
    zn_scr[...] = (zc * lax.rsqrt(var + NORM_EPS) * lng_ref[...] + lnb_ref[...]).astype(BF16)
    u_scr[...] = jax.nn.gelu(gu_ref[...])
    gd = GM_WIDTH // GM_GROUPS
    for c in range(o_ref.shape[0] // GM_CHUNK):
        rs = slice(c * GM_CHUNK, (c + 1) * GM_CHUNK)
        for g in range(GM_GROUPS):
            cs = slice(g * gd, (g + 1) * gd)
            mixed = jnp.dot(ws_ref[g], zn_scr[rs, cs], preferred_element_type=F32) + bs_ref[g]
            o_ref[rs, cs] = (u_scr[rs, cs].astype(F32) * mixed).astype(BF16)


def spatial_gating(proj, ln_g, ln_b, w_s, b_s_b, *, bm):
    t = proj.shape[0]
    return pl.pallas_call(
        _gmlp_kernel,
        grid=(t // bm,),
        in_specs=[
            pl.BlockSpec((bm, GM_WIDTH), lambda i: (i, COL_GU)),
            pl.BlockSpec((bm, GM_WIDTH), lambda i: (i, COL_GZ)),
            pl.BlockSpec((1, GM_WIDTH), lambda i: (0, 0)),
            pl.BlockSpec((1, GM_WIDTH), lambda i: (0, 0)),
            pl.BlockSpec((GM_GROUPS, GM_CHUNK, GM_CHUNK), lambda i: (0, 0, 0)),
            pl.BlockSpec((GM_GROUPS, GM_CHUNK, GM_WIDTH // GM_GROUPS), lambda i: (0, 0, 0)),
        ],
        out_specs=pl.BlockSpec((bm, GM_WIDTH), lambda i: (i, 0)),
        out_shape=jax.ShapeDtypeStruct((t, GM_WIDTH), BF16),
        scratch_shapes=[pltpu.VMEM((bm, GM_WIDTH), BF16), pltpu.VMEM((bm, GM_WIDTH), BF16)],
        compiler_params=_cparams("parallel"),
        name="spatial_gating",
    )(proj, proj, ln_g.reshape(1, -1), ln_b.reshape(1, -1), w_s, b_s_b)


def _sigmoid(x):
    return 1.0 / (1.0 + jnp.exp(-x))


def _merge_kernel(att_ref, gm_ref, ga_ref, gb_ref, x_ref, wa_ref, wg_ref, wo_ref, y_ref):
    a = jnp.dot(att_ref[...], wa_ref[...], preferred_element_type=F32)
    b = jnp.dot(gm_ref[...], wg_ref[...], preferred_element_type=F32)
    merged = _sigmoid(ga_ref[...].astype(F32)) * a + _sigmoid(gb_ref[...].astype(F32)) * b
    y_ref[...] = x_ref[...] + jnp.dot(merged.astype(BF16), wo_ref[...], preferred_element_type=F32)


def merge_project(att, gm, proj, x, wa, wg, wo, *, bm):
    t, d = x.shape
    const = dict(pipeline_mode=pl.Buffered(1))
    return pl.pallas_call(
        _merge_kernel,
        grid=(t // bm,),
        in_specs=[
            pl.BlockSpec((bm, ATT_WIDTH), lambda i: (i, 0)),
            pl.BlockSpec((bm, GM_WIDTH), lambda i: (i, 0)),
            pl.BlockSpec((bm, d), lambda i: (i, COL_GA)),
            pl.BlockSpec((bm, d), lambda i: (i, COL_GB)),
            pl.BlockSpec((bm, d), lambda i: (i, 0)),
            pl.BlockSpec((ATT_WIDTH, d), lambda i: (0, 0), **const),
            pl.BlockSpec((GM_WIDTH, d), lambda i: (0, 0), **const),
            pl.BlockSpec((d, d), lambda i: (0, 0), **const),
        ],
        out_specs=pl.BlockSpec((bm, d), lambda i: (i, 0)),
        out_shape=jax.ShapeDtypeStruct((t, d), F32),
        compiler_params=_cparams("parallel"),
        name="merge_project",
    )(att, gm, proj, proj, x, wa, wg, wo)


def _cross_kernel(meta_ref, x_ref, g_ref, wq_ref, kv_ref, wo_ref, y_ref):
    x = x_ref[...]
    xn = _rms(x, g_ref[...]).astype(BF16)
    q = jnp.dot(xn, wq_ref[...], preferred_element_type=F32).astype(BF16)
    scale = MEM_HEAD_DIM ** -0.5
    outs = []
    for h in range(MEM_HEADS):
        sl = slice(h * MEM_HEAD_DIM, (h + 1) * MEM_HEAD_DIM)
        kh = kv_ref[0, :, sl]
        vh = kv_ref[0, :, MEM_WIDTH + h * MEM_HEAD_DIM:MEM_WIDTH + (h + 1) * MEM_HEAD_DIM]
        s = _dot_nt(q[:, sl], kh) * scale
        m = jnp.max(s, axis=-1, keepdims=True)
        e = jnp.exp(s - m)
        p = (e / jnp.sum(e, axis=-1, keepdims=True)).astype(BF16)
        outs.append(jnp.dot(p, vh, preferred_element_type=F32).astype(BF16))
    o = jnp.concatenate(outs, axis=-1)
    y_ref[...] = x + jnp.dot(o, wo_ref[...], preferred_element_type=F32)


def cross_attention(x, meta, g, wq, kv, wo, *, bm):
    t, d = x.shape
    grid_spec = pltpu.PrefetchScalarGridSpec(
        num_scalar_prefetch=1,
        grid=(t // bm,),
        in_specs=[
            pl.BlockSpec((bm, d), lambda i, m: (i, 0)),
            pl.BlockSpec((1, d), lambda i, m: (0, 0)),
            pl.BlockSpec((d, MEM_WIDTH), lambda i, m: (0, 0)),
            pl.BlockSpec((1, MEM_TOKENS, 2 * MEM_WIDTH), lambda i, m: (m[3, i], 0, 0)),
            pl.BlockSpec((MEM_WIDTH, d), lambda i, m: (0, 0)),
        ],
        out_specs=pl.BlockSpec((bm, d), lambda i, m: (i, 0)),
    )
    return pl.pallas_call(
        _cross_kernel,
        grid_spec=grid_spec,
        out_shape=jax.ShapeDtypeStruct((t, d), F32),
        compiler_params=_cparams("parallel"),
        name="cross_attention",
    )(meta, x, g.reshape(1, d), wq, kv, wo)


N_TOP = PEER_TOPK + 1
CAND = [(a, b) for a in range(N_TOP) for b in range(N_TOP // (a + 1))]
CAND_ROWS = 8 * SUBLANES
assert len(CAND) <= CAND_ROWS
XPOSE_COLS = 256


def _oddeven_merge_sort(n):
    def merge(lo, hi, r):
        step = r * 2
        if step < hi - lo:
            yield from merge(lo, hi, step)
            yield from merge(lo + r, hi, step)
            yield from [(i, i + r) for i in range(lo + r, hi - r, step)]
        else:
            yield (lo, lo + r)

    def sort(lo, hi):
        if hi - lo >= 1:
            mid = lo + (hi - lo) // 2
            yield from sort(lo, mid)
            yield from sort(mid + 1, hi)
            yield from merge(lo, hi, 1)

    return list(sort(0, n - 1))


def _top_values(s, n_top, one_at_a_time=False):
    nblk = s.shape[0] // SUBLANES
    v = [s[d * SUBLANES:(d + 1) * SUBLANES, :] for d in range(nblk)]
    for a, b in _oddeven_merge_sort(nblk):
        v[a], v[b] = jnp.maximum(v[a], v[b]), jnp.minimum(v[a], v[b])
    sub = lax.broadcasted_iota(jnp.int32, v[0].shape, 0)
    vals = []
    for k in range(n_top):
        m = jnp.max(v[0], axis=0, keepdims=True)
        vals.append(m)
        remaining = n_top - 1 - k
        if remaining == 0:
            break
        hit = v[0] == m
        if one_at_a_time:
            first = jnp.min(jnp.where(hit, sub, SUBLANES), axis=0, keepdims=True)
            hit = sub == first
        for d in range(min(nblk, remaining)):
            v[d] = jnp.where(hit, v[d + 1] if d + 1 < nblk else NEG_INF, v[d])
    return vals


def _count_above(vals, x, strict):
    assert len(vals) == 16

    def above(v):
        return (v > x) if strict else (v >= x)

    bits = []

    def pivot(level, lo, size, depth=0):
        mid = lo + size // 2
        if depth == level:
            return vals[mid]
        return jnp.where(bits[depth], pivot(level, mid + 1, size // 2, depth + 1),
                         pivot(level, lo, size // 2, depth + 1))

    count = jnp.where(above(vals[15]), 1.0, 0.0)
    for level in range(4):
        bits.append(above(pivot(level, 0, 15)))
        count = count + jnp.where(bits[level], float(8 >> level), 0.0)
    return count


def _router_kernel(x_ref, g_ref, wqt_ref, k1_ref, k2_ref, xnt_ref, r2_ref, bz_ref, cnt_ref, a_ref,
                   qt_scr, c_scr):
    d = x_ref.shape[1]
    x = x_ref[...]
    inv = lax.rsqrt(jnp.mean(x * x, axis=-1, keepdims=True) + NORM_EPS)
    for c0 in range(0, d, XPOSE_COLS):
        cols = slice(c0, c0 + XPOSE_COLS)
        xn_c = x_ref[:, cols] * inv * g_ref[:, cols]
        xnt_ref[cols, :] = xn_c.T.astype(BF16)
    qt_scr[...] = jnp.dot(wqt_ref[...], xnt_ref[...],
                          preferred_element_type=F32).astype(BF16)
    c_scr[...] = jnp.full(c_scr.shape, NEG_INF, F32)

    def head(h, carry):
        off = pl.multiple_of(h * PEER_QDIM, PEER_QDIM)
        s1 = jnp.dot(k1_ref[...], qt_scr[pl.ds(off, PEER_HALF), :], preferred_element_type=F32)
        s2 = jnp.dot(k2_ref[...], qt_scr[pl.ds(off + PEER_HALF, PEER_HALF), :],
                     preferred_element_type=F32)
        v1 = _top_values(s1, N_TOP)
        v2 = _top_values(s2, N_TOP)
        for r, (a, b) in enumerate(CAND):
            c_scr[r:r + 1, :] = v1[a] + v2[b]
        c0 = c_scr[...]
        csort = _top_values(c0, N_TOP, one_at_a_time=True)
        thr = 0.5 * (csort[PEER_TOPK - 1] + csort[PEER_TOPK])
        top = v1[0] + v2[0]
        z = jnp.sum(jnp.where(c0 >= thr, jnp.exp(c0 - top), 0.0), axis=0, keepdims=True)
        r2_ref[h] = _count_above(v2[:PEER_TOPK], s2, strict=True).astype(BF16)
        bz_ref[h] = (jnp.exp(s2 - v2[0]) / z).astype(BF16)
        cnt_ref[h] = _count_above(v2[:PEER_TOPK], thr - s1, strict=False)
        a_ref[h] = jnp.exp(s1 - v1[0])
        return carry

    lax.fori_loop(0, PEER_HEADS, head, 0)


def peer_router(x, g, wqt, k1, k2, *, bm):
    t, d = x.shape
    head_spec = pl.BlockSpec((PEER_HEADS, N_KEYS, bm), lambda i: (0, 0, i))
    return pl.pallas_call(
        _router_kernel,
        grid=(t // bm,),
        in_specs=[
            pl.BlockSpec((bm, d), lambda i: (i, 0)),
            pl.BlockSpec((1, d), lambda i: (0, 0)),
            pl.BlockSpec((PEER_HEADS * PEER_QDIM, d), lambda i: (0, 0)),
            pl.BlockSpec((N_KEYS, PEER_HALF), lambda i: (0, 0)),
            pl.BlockSpec((N_KEYS, PEER_HALF), lambda i: (0, 0)),
        ],
        out_specs=[pl.BlockSpec((d, bm), lambda i: (0, i)), head_spec, head_spec, head_spec, head_spec],
        out_shape=[jax.ShapeDtypeStruct((d, t), BF16),
                   jax.ShapeDtypeStruct((PEER_HEADS, N_KEYS, t), BF16),
                   jax.ShapeDtypeStruct((PEER_HEADS, N_KEYS, t), BF16),
                   jax.ShapeDtypeStruct((PEER_HEADS, N_KEYS, t), F32),
                   jax.ShapeDtypeStruct((PEER_HEADS, N_KEYS, t), F32)],
        scratch_shapes=[pltpu.VMEM((PEER_HEADS * PEER_QDIM, bm), BF16), pltpu.VMEM((CAND_ROWS, bm), F32)],
        compiler_params=_cparams("parallel"),
        name="peer_router",
    )(x, g.reshape(1, d), wqt, k1, k2)


EXPERT_TILE = 2048
EXPERT_VMEM_LIMIT = 62 * 1024 * 1024
EXPERT_CHUNK = 256
KEYS_PER_TILE = EXPERT_TILE // N_KEYS
KEYS_PER_CHUNK = EXPERT_CHUNK // N_KEYS
CHUNKS_PER_TILE = EXPERT_TILE // EXPERT_CHUNK
TOKEN_GROUP = 2 * LANES
assert KEYS_PER_TILE % SUBLANES == 0


def _expert_kernel(xnt_ref, u_ref, vt_ref, r2_ref, bz_ref, cnt_ref, a_ref, x_hbm, y_ref, p_scr, acc_scr,
                   x_sem):
    e = pl.program_id(1)
    bm = y_ref.shape[0]

    def x_copy():
        row0 = pl.multiple_of(pl.program_id(0) * bm, bm)
        return pltpu.make_async_copy(x_hbm.at[pl.ds(row0, bm), :], y_ref, x_sem)

    @pl.when(e == 0)
    def _():
        acc_scr[...] = jnp.zeros(acc_scr.shape, F32)
        x_copy().start()

    for c in range(CHUNKS_PER_TILE):
        crow = slice(c * EXPERT_CHUNK, (c + 1) * EXPERT_CHUNK)
        h = jnp.dot(u_ref[crow, :], xnt_ref[...], preferred_element_type=F32)
        for kk in range(KEYS_PER_CHUNK):
            il = c * KEYS_PER_CHUNK + kk
            rows = slice(il * N_KEYS, (il + 1) * N_KEYS)
            for tg in range(bm // TOKEN_GROUP):
                ts = slice(tg * TOKEN_GROUP, (tg + 1) * TOKEN_GROUP)
                g = jax.nn.gelu(h[kk * N_KEYS:(kk + 1) * N_KEYS, ts].astype(BF16))
                p = jnp.zeros((N_KEYS, TOKEN_GROUP), BF16)
                for hd in range(PEER_HEADS):
                    cb = jnp.broadcast_to(cnt_ref[hd, il:il + 1, ts], (N_KEYS, TOKEN_GROUP)).astype(BF16)
                    ab = jnp.broadcast_to(a_ref[hd, il:il + 1, ts], (N_KEYS, TOKEN_GROUP)).astype(BF16)
                    p = p + jnp.where(r2_ref[hd, :, ts] < cb, bz_ref[hd, :, ts], 0.0) * ab
                p_scr[rows, ts] = p * g
    acc_scr[...] += jnp.dot(vt_ref[...], p_scr[...], preferred_element_type=F32)

    @pl.when(e == pl.num_programs(1) - 1)
    def _():
        x_copy().wait()
        for c0 in range(0, y_ref.shape[1], XPOSE_COLS):
            cols = slice(c0, c0 + XPOSE_COLS)
            y_ref[:, cols] += acc_scr[cols, :].T


def peer_experts(xnt, u, vt, r2, bz, cnt, a, x, *, bm):
    t, d = x.shape
    assert bm % TOKEN_GROUP == 0
    head_full = pl.BlockSpec((PEER_HEADS, N_KEYS, bm), lambda i, e: (0, 0, i))
    head_rows = pl.BlockSpec((PEER_HEADS, KEYS_PER_TILE, bm), lambda i, e: (0, e, i))
    return pl.pallas_call(
        _expert_kernel,
        grid=(t // bm, u.shape[0] // EXPERT_TILE),
        in_specs=[
            pl.BlockSpec((d, bm), lambda i, e: (0, i)),
            pl.BlockSpec((EXPERT_TILE, d), lambda i, e: (e, 0)),
            pl.BlockSpec((d, EXPERT_TILE), lambda i, e: (0, e)),
            head_full, head_full, head_rows, head_rows,
            pl.BlockSpec(memory_space=pl.ANY),
        ],
        out_specs=pl.BlockSpec((bm, d), lambda i, e: (i, 0)),
        out_shape=jax.ShapeDtypeStruct((t, d), F32),
        scratch_shapes=[
            pltpu.VMEM((EXPERT_TILE, bm), BF16),
            pltpu.VMEM((d, bm), F32),
            pltpu.SemaphoreType.DMA(()),
        ],
        compiler_params=pltpu.CompilerParams(dimension_semantics=("parallel", "arbitrary"),
                                             vmem_limit_bytes=EXPERT_VMEM_LIMIT),
        name="peer_experts",
    )(xnt, u, vt, r2, bz, cnt, a, x)


def _final_norm_kernel(x_ref, g_ref, y_ref):
    y_ref[...] = _rms(x_ref[...], g_ref[...])


def final_norm_slice(x, g, start, rows, *, bm):
    d = x.shape[1]
    off = start // bm
    return pl.pallas_call(
        _final_norm_kernel,
        grid=(rows // bm,),
        in_specs=[pl.BlockSpec((bm, d), lambda i: (i + off, 0)), pl.BlockSpec((1, d), lambda i: (0, 0))],
        out_specs=pl.BlockSpec((bm, d), lambda i: (i, 0)),
        out_shape=jax.ShapeDtypeStruct((rows, d), F32),
        compiler_params=_cparams("parallel"),
        name="final_norm",
    )(x, g.reshape(1, d))


def _block_meta(groups, blk):
    pos, first, last, batch = [], [], [], []
    b0 = 0
    for (b, s) in groups:
        per = s // blk
        for bi in range(b):
            for j in range(per):
                pos.append(j)
                first.append(int(j == 0))
                last.append(int(j == per - 1))
                batch.append(b0 + bi)
        b0 += b
    return jnp.asarray(np.array([pos, first, last, batch], dtype=np.int32))


def _rope_table(max_pos):
    half = HEAD_DIM // 2
    inv = ROPE_THETA ** (-jnp.arange(half, dtype=F32) / half)
    ang = jnp.arange(max_pos, dtype=F32)[:, None] * inv[None, :]
    cos, sin = jnp.cos(ang), jnp.sin(ang)
    return jnp.concatenate([cos, cos, -sin, sin], axis=-1)


def _pick(t, pref):
    while t % pref:
        pref //= 2
    return pref


def _trunk(xs, mems, mix_norm, w_in, attn_sink, gm_ln_g, gm_ln_b, gm_w_s, gm_b_s, w_att_proj, w_gm_proj,
           w_out, cross_norm, mem_norm, w_q_mem, w_kv_mem, w_o_mem, peer_norm, w_q_peer, peer_k1, peer_k2,
           expert_u, expert_v, final_norm):
    groups = [(x.shape[0], x.shape[1]) for x in xs]
    depth = w_in.shape[0]
    for (_, s) in groups:
        assert s % ATT_BLOCK == 0
    x = jnp.concatenate([x.reshape(-1, D_MODEL) for x in xs], axis=0)
    mem = jnp.concatenate([m.reshape(-1, D_MODEL) for m in mems], axis=0)
    t = x.shape[0]
    n_mem_rows = mem.shape[0]
    meta = _block_meta(groups, ATT_BLOCK)
    cs_table = _rope_table(max(s for _, s in groups))

    bm_big = _pick(t, 1024)
    bm_mid = _pick(t, 512)
    bm_small = _pick(t, 256)
    in_splits = np.cumsum([ATT_WIDTH, KV_WIDTH, KV_WIDTH, GM_WIDTH, GM_WIDTH, D_MODEL]).tolist()

    for l in range(depth):
        q_w, k_w, v_w, gu_w, gz_w, ga_w, gb_w = jnp.split(w_in[l], in_splits, axis=1)
        w_in_p = jnp.concatenate([q_w, gu_w, gz_w, ga_w, gb_w, k_w, v_w], axis=1).astype(BF16)
        sink_col = jnp.broadcast_to((attn_sink[l].astype(F32) * LOG2_E)[:, None, None],
                                    (N_Q_HEADS, WINDOW, SINK_LANES))
        b_s_b = jnp.broadcast_to(gm_b_s[l].astype(F32)[:, :, None],
                                 (GM_GROUPS, GM_CHUNK, GM_WIDTH // GM_GROUPS))

        proj = norm_matmul(x, mix_norm[l], w_in_p, bm=bm_big, bn=1024)
        att = banded_attention(proj, meta, cs_table, sink_col)
        gm = spatial_gating(proj, gm_ln_g[l], gm_ln_b[l], gm_w_s[l].astype(BF16), b_s_b, bm=ATT_BLOCK)
        x = merge_project(att, gm, proj, x, w_att_proj[l].astype(BF16), w_gm_proj[l].astype(BF16),
                          w_out[l].astype(BF16), bm=bm_small)

        kv = norm_matmul(mem, mem_norm[l], w_kv_mem[l].astype(BF16), bm=_pick(n_mem_rows, 512), bn=1024)
        kv = kv.reshape(-1, MEM_TOKENS, 2 * MEM_WIDTH)
        x = cross_attention(x, meta, cross_norm[l], w_q_mem[l].astype(BF16), kv, w_o_mem[l].astype(BF16),
                            bm=ATT_BLOCK)

        xnt, r2, bz, cnt, a = peer_router(x, peer_norm[l], w_q_peer[l].T.astype(BF16),
                                          peer_k1[l].astype(BF16), peer_k2[l].astype(BF16), bm=bm_mid)
        x = peer_experts(xnt, expert_u[l].astype(BF16), expert_v[l].T.astype(BF16), r2, bz, cnt, a, x,
                         bm=bm_mid)

    outs = []
    start = 0
    for (b, s) in groups:
        rows = b * s
        y = final_norm_slice(x, final_norm, start, rows, bm=_pick(int(np.gcd(rows, start)), 1024))
        outs.append(y.reshape(b, s, D_MODEL))
        start += rows
    return tuple(outs)


def kernel(x_prompt, x_sample, mem_prompt, mem_sample, mix_norm, w_in, attn_sink, gm_ln_g, gm_ln_b, gm_w_s, gm_b_s, w_att_proj, w_gm_proj, w_out, cross_norm, mem_norm, w_q_mem, w_kv_mem, w_o_mem, peer_norm, w_q_peer, peer_k1, peer_k2, expert_u, expert_v, final_norm):
    return _trunk((x_prompt, x_sample), (mem_prompt, mem_sample), mix_norm, w_in, attn_sink, gm_ln_g,
                  gm_ln_b, gm_w_s, gm_b_s, w_att_proj, w_gm_proj, w_out, cross_norm, mem_norm, w_q_mem,
                  w_kv_mem, w_o_mem, peer_norm, w_q_peer, peer_k1, peer_k2, expert_u, expert_v, final_norm)
```

```python
import functools

import numpy as np
import jax
import jax.numpy as jnp
from jax import lax
from jax.experimental import pallas as pl
from jax.experimental.pallas import tpu as pltpu

F32 = jnp.float32
BF16 = jnp.bfloat16

D_MODEL = 2048
HEAD_DIM = 128
N_Q_HEADS = 16
N_KV_HEADS = 4
Q_PER_KV = N_Q_HEADS // N_KV_HEADS
ATT_WIDTH = N_Q_HEADS * HEAD_DIM
KV_WIDTH = N_KV_HEADS * HEAD_DIM
WINDOW = 128
ROPE_THETA = 10000.0
GM_WIDTH = 2048
GM_GROUPS = 16
GM_CHUNK = 128
MEM_TOKENS = 256
MEM_HEADS = 4
MEM_HEAD_DIM = 128
MEM_WIDTH = MEM_HEADS * MEM_HEAD_DIM
PEER_HEADS = 8
PEER_QDIM = 256
PEER_HALF = PEER_QDIM // 2
N_KEYS = 128
N_EXPERTS = N_KEYS * N_KEYS
PEER_TOPK = 16
NORM_EPS = 1e-6
NEG_INF = -1e30
LOG2_E = 1.4426950408889634

COL_Q, COL_GU, COL_GZ, COL_GA, COL_GB = 0, 1, 2, 3, 4
COL_K = (ATT_WIDTH + 4 * D_MODEL) // KV_WIDTH
COL_V = COL_K + 1

LANES = 128
SUBLANES = 8
ATT_BLOCK = 512
SINK_LANES = LANES
HEADS_PER_PASS = 2
VMEM_LIMIT = 56 * 1024 * 1024


def _cparams(*sem):
    return pltpu.CompilerParams(dimension_semantics=sem, vmem_limit_bytes=VMEM_LIMIT)


def _rms(x, g):
    return x * lax.rsqrt(jnp.mean(x * x, axis=-1, keepdims=True) + NORM_EPS) * g


def _dot_nt(a, b):
    return lax.dot_general(a, b, (((1,), (1,)), ((), ())), preferred_element_type=F32)


def _row_parts(xs, bm):
    xa, xb = (xs[0], xs[0]) if len(xs) == 1 else xs
    assert all(x.shape[0] % bm == 0 for x in xs)
    return xa, xb, xa.shape[0] // bm, sum(x.shape[0] for x in xs)


def _part_specs(bm, d, na, ndim_grid):
    if ndim_grid == 1:
        return [pl.BlockSpec((bm, d), lambda i: (jnp.minimum(i, na - 1), 0)),
                pl.BlockSpec((bm, d), lambda i: (jnp.maximum(i - na, 0), 0))]
    return [pl.BlockSpec((bm, d), lambda i, j: (jnp.minimum(i, na - 1), 0)),
            pl.BlockSpec((bm, d), lambda i, j: (jnp.maximum(i - na, 0), 0))]


def _norm_matmul_kernel(xa_ref, xb_ref, g_ref, w_ref, o_ref, xn_ref, *, na):
    i = pl.program_id(0)
    first = pl.program_id(1) == 0

    @pl.when(jnp.logical_and(first, i < na))
    def _():
        xn_ref[...] = _rms(xa_ref[...], g_ref[...]).astype(BF16)

    @pl.when(jnp.logical_and(first, i >= na))
    def _():
        xn_ref[...] = _rms(xb_ref[...], g_ref[...]).astype(BF16)

    o_ref[...] = jnp.dot(xn_ref[...], w_ref[...], preferred_element_type=F32).astype(o_ref.dtype)


def norm_matmul(xs, g, w, *, bm, bn):
    xa, xb, na, t = _row_parts(xs, bm)
    d = xa.shape[1]
    n = w.shape[1]
    assert n % bn == 0
    return pl.pallas_call(
        functools.partial(_norm_matmul_kernel, na=na),
        grid=(t // bm, n // bn),
        in_specs=_part_specs(bm, d, na, 2) + [
            pl.BlockSpec((1, d), lambda i, j: (0, 0)),
            pl.BlockSpec((d, bn), lambda i, j: (0, j)),
        ],
        out_specs=pl.BlockSpec((bm, bn), lambda i, j: (i, j)),
        out_shape=jax.ShapeDtypeStruct((t, n), BF16),
        scratch_shapes=[pltpu.VMEM((bm, d), BF16)],
        compiler_params=_cparams("parallel", "arbitrary"),
        name="norm_matmul",
    )(xa, xb, g.reshape(1, d), w)


def _attn_kernel(meta_ref, q_ref, kc_ref, kp_ref, kn_ref, vc_ref, vp_ref, vn_ref,
                 csc_ref, csp_ref, csn_ref, sink_ref, o_ref, q_scr, k_scr, v_scr, o_scr):
    n = pl.program_id(0)
    nb = ATT_BLOCK // WINDOW
    band = 3 * WINDOW
    k_lo = jnp.where(meta_ref[1, n] == 1, WINDOW, 0)
    k_hi = jnp.where(meta_ref[2, n] == 1, 2 * WINDOW, band)

    def rope(x, cs):
        return x * cs[:, :HEAD_DIM] + pltpu.roll(x, HEAD_DIM // 2, 1) * cs[:, HEAD_DIM:]

    cs_c = csc_ref[...]
    for hq in range(N_Q_HEADS):
        sl = slice(hq * HEAD_DIM, (hq + 1) * HEAD_DIM)
        q_scr[hq] = rope(q_ref[:, sl].astype(F32), cs_c).astype(BF16)
    cs_p = csp_ref[...]
    cs_n = csn_ref[...]
    for h in range(N_KV_HEADS):
        sl = slice(h * HEAD_DIM, (h + 1) * HEAD_DIM)
        k_scr[h, 0:WINDOW, :] = rope(kp_ref[:, sl].astype(F32), cs_p).astype(BF16)
        k_scr[h, WINDOW:WINDOW + ATT_BLOCK, :] = rope(kc_ref[:, sl].astype(F32), cs_c).astype(BF16)
        k_scr[h, WINDOW + ATT_BLOCK:, :] = rope(kn_ref[:, sl].astype(F32), cs_n).astype(BF16)
        v_scr[h, 0:WINDOW, :] = vp_ref[:, sl]
        v_scr[h, WINDOW:WINDOW + ATT_BLOCK, :] = vc_ref[:, sl]
        v_scr[h, WINDOW + ATT_BLOCK:, :] = vn_ref[:, sl]

    qi = lax.broadcasted_iota(jnp.int32, (WINDOW, band), 0)
    kc = lax.broadcasted_iota(jnp.int32, (WINDOW, band), 1)
    rel = jnp.abs(qi + WINDOW - kc) <= WINDOW
    scale = HEAD_DIM ** -0.5 * LOG2_E

    def band_step(r, carry):
        r0 = pl.multiple_of(r * WINDOW, WINDOW)
        lo = jnp.where(r == 0, k_lo, 0)
        hi = jnp.where(r == nb - 1, k_hi, band)
        bias = jnp.where(jnp.logical_and(rel, jnp.logical_and(kc >= lo, kc < hi)), 0.0, NEG_INF)
        for h0 in range(0, N_KV_HEADS, HEADS_PER_PASS):
            heads = range(h0, h0 + HEADS_PER_PASS)
            kbs = {h: k_scr[h, pl.ds(r0, band), :] for h in heads}
            vbs = {h: v_scr[h, pl.ds(r0, band), :] for h in heads}
            scores = {(h, g): _dot_nt(q_scr[Q_PER_KV * h + g, pl.ds(r0, WINDOW), :], kbs[h])
                      for h in heads for g in range(Q_PER_KV)}
            for h in heads:
                for g in range(Q_PER_KV):
                    hq = Q_PER_KV * h + g
                    s = scores[(h, g)] * scale + bias
                    sk = sink_ref[hq]
                    m = jnp.maximum(jnp.max(s, axis=-1, keepdims=True), sk)
                    e = jnp.concatenate([jnp.exp2(s[:, j * WINDOW:(j + 1) * WINDOW] - m) for j in range(3)],
                                        axis=-1)
                    den = jnp.sum(e, axis=-1, keepdims=True) + jnp.exp2(sk - m)
                    o = jnp.dot(e.astype(BF16), vbs[h], preferred_element_type=F32) * (1.0 / den)
                    o_scr[hq, pl.ds(r0, WINDOW), :] = o.astype(BF16)
        return carry

    lax.fori_loop(0, nb, band_step, 0)
    for hq in range(N_Q_HEADS):
        o_ref[:, hq * HEAD_DIM:(hq + 1) * HEAD_DIM] = o_scr[hq]


def banded_attention(proj, meta, cs_table, sink_col):
    t = proj.shape[0]
    nblk = t // ATT_BLOCK
    sub = ATT_BLOCK // WINDOW
    n_small = t // WINDOW
    n_pos_small = cs_table.shape[0] // WINDOW
    grid_spec = pltpu.PrefetchScalarGridSpec(
        num_scalar_prefetch=1,
        grid=(nblk,),
        in_specs=[
            pl.BlockSpec((ATT_BLOCK, ATT_WIDTH), lambda n, m: (n, COL_Q)),
            pl.BlockSpec((ATT_BLOCK, KV_WIDTH), lambda n, m: (n, COL_K)),
            pl.BlockSpec((WINDOW, KV_WIDTH), lambda n, m: (jnp.maximum(n * sub - 1, 0), COL_K)),
            pl.BlockSpec((WINDOW, KV_WIDTH), lambda n, m: (jnp.minimum(n * sub + sub, n_small - 1), COL_K)),
            pl.BlockSpec((ATT_BLOCK, KV_WIDTH), lambda n, m: (n, COL_V)),
            pl.BlockSpec((WINDOW, KV_WIDTH), lambda n, m: (jnp.maximum(n * sub - 1, 0), COL_V)),
            pl.BlockSpec((WINDOW, KV_WIDTH), lambda n, m: (jnp.minimum(n * sub + sub, n_small - 1), COL_V)),
            pl.BlockSpec((ATT_BLOCK, 2 * HEAD_DIM), lambda n, m: (m[0, n], 0)),
            pl.BlockSpec((WINDOW, 2 * HEAD_DIM), lambda n, m: (jnp.maximum(m[0, n] * sub - 1, 0), 0)),
            pl.BlockSpec((WINDOW, 2 * HEAD_DIM),
                         lambda n, m: (jnp.minimum(m[0, n] * sub + sub, n_pos_small - 1), 0)),
            pl.BlockSpec((N_Q_HEADS, WINDOW, SINK_LANES), lambda n, m: (0, 0, 0)),
        ],
        out_specs=pl.BlockSpec((ATT_BLOCK, ATT_WIDTH), lambda n, m: (n, 0)),
        scratch_shapes=[
            pltpu.VMEM((N_Q_HEADS, ATT_BLOCK, HEAD_DIM), BF16),
            pltpu.VMEM((N_KV_HEADS, ATT_BLOCK + 2 * WINDOW, HEAD_DIM), BF16),
            pltpu.VMEM((N_KV_HEADS, ATT_BLOCK + 2 * WINDOW, HEAD_DIM), BF16),
            pltpu.VMEM((N_Q_HEADS, ATT_BLOCK, HEAD_DIM), BF16),
        ],
    )
    return pl.pallas_call(
        _attn_kernel,
        grid_spec=grid_spec,
        out_shape=jax.ShapeDtypeStruct((t, ATT_WIDTH), BF16),
        compiler_params=_cparams("parallel"),
        name="banded_attention",
    )(meta, proj, proj, proj, proj, proj, proj, proj, cs_table, cs_table, cs_table, sink_col)


def _gmlp_kernel(gu_ref, gz_ref, lng_ref, lnb_ref, ws_ref, bs_ref, o_ref, zn_scr, u_scr):
    z = jax.nn.gelu(gz_ref[...].astype(F32))
    mu = jnp.mean(z, axis=-1, keepdims=True)
    zc = z - mu
    var = jnp.mean(zc * zc, axis=-1, keepdims=True)
    zn_scr[...] = (zc * lax.rsqrt(var + NORM_EPS) * lng_ref[...] + lnb_ref[...]).astype(BF16)
    u_scr[...] = jax.nn.gelu(gu_ref[...])
    gd = GM_WIDTH // GM_GROUPS
    for c in range(o_ref.shape[0] // GM_CHUNK):
        rs = slice(c * GM_CHUNK, (c + 1) * GM_CHUNK)
        for g in range(GM_GROUPS):
            cs = slice(g * gd, (g + 1) * gd)
            mixed = jnp.dot(ws_ref[g], zn_scr[rs, cs], preferred_element_type=F32) + bs_ref[g]
            o_ref[rs, cs] = (u_scr[rs, cs].astype(F32) * mixed).astype(BF16)


def spatial_gating(proj, ln_g, ln_b, w_s, b_s_b, *, bm):
    t = proj.shape[0]
    return pl.pallas_call(
        _gmlp_kernel,
        grid=(t // bm,),
        in_specs=[
            pl.BlockSpec((bm, GM_WIDTH), lambda i: (i, COL_GU)),
            pl.BlockSpec((bm, GM_WIDTH), lambda i: (i, COL_GZ)),
            pl.BlockSpec((1, GM_WIDTH), lambda i: (0, 0)),
            pl.BlockSpec((1, GM_WIDTH), lambda i: (0, 0)),
            pl.BlockSpec((GM_GROUPS, GM_CHUNK, GM_CHUNK), lambda i: (0, 0, 0)),
            pl.BlockSpec((GM_GROUPS, GM_CHUNK, GM_WIDTH // GM_GROUPS), lambda i: (0, 0, 0)),
        ],
        out_specs=pl.BlockSpec((bm, GM_WIDTH), lambda i: (i, 0)),
        out_shape=jax.ShapeDtypeStruct((t, GM_WIDTH), BF16),
        scratch_shapes=[pltpu.VMEM((bm, GM_WIDTH), BF16), pltpu.VMEM((bm, GM_WIDTH), BF16)],
        compiler_params=_cparams("parallel"),
        name="spatial_gating",
    )(proj, proj, ln_g.reshape(1, -1), ln_b.reshape(1, -1), w_s, b_s_b)


def _sigmoid(x):
    return 1.0 / (1.0 + jnp.exp(-x))


def _merge_kernel(att_ref, gm_ref, ga_ref, gb_ref, xa_ref, xb_ref, wa_ref, wg_ref, wo_ref, y_ref, *, na):
    i = pl.program_id(0)

    @pl.when(i < na)
    def _():
        y_ref[...] = xa_ref[...]

    @pl.when(i >= na)
    def _():
        y_ref[...] = xb_ref[...]

    a = jnp.dot(att_ref[...], wa_ref[...], preferred_element_type=F32)
    b = jnp.dot(gm_ref[...], wg_ref[...], preferred_element_type=F32)
    merged = _sigmoid(ga_ref[...].astype(F32)) * a + _sigmoid(gb_ref[...].astype(F32)) * b
    y_ref[...] += jnp.dot(merged.astype(BF16), wo_ref[...], preferred_element_type=F32)


def merge_project(att, gm, proj, xs, wa, wg, wo, *, bm):
    xa, xb, na, t = _row_parts(xs, bm)
    d = xa.shape[1]
    const = dict(pipeline_mode=pl.Buffered(1))
    return pl.pallas_call(
        functools.partial(_merge_kernel, na=na),
        grid=(t // bm,),
        in_specs=[
            pl.BlockSpec((bm, ATT_WIDTH), lambda i: (i, 0)),
            pl.BlockSpec((bm, GM_WIDTH), lambda i: (i, 0)),
            pl.BlockSpec((bm, d), lambda i: (i, COL_GA)),
            pl.BlockSpec((bm, d), lambda i: (i, COL_GB)),
        ] + _part_specs(bm, d, na, 1) + [
            pl.BlockSpec((ATT_WIDTH, d), lambda i: (0, 0), **const),
            pl.BlockSpec((GM_WIDTH, d), lambda i: (0, 0), **const),
            pl.BlockSpec((d, d), lambda i: (0, 0), **const),
        ],
        out_specs=pl.BlockSpec((bm, d), lambda i: (i, 0)),
        out_shape=jax.ShapeDtypeStruct((t, d), F32),
        compiler_params=_cparams("parallel"),
        name="merge_project",
    )(att, gm, proj, proj, xa, xb, wa, wg, wo)


def _cross_kernel(meta_ref, x_ref, g_ref, wq_ref, kv_ref, wo_ref, y_ref):
    x = x_ref[...]
    xn = _rms(x, g_ref[...]).astype(BF16)
    q = jnp.dot(xn, wq_ref[...], preferred_element_type=F32).astype(BF16)
    scale = MEM_HEAD_DIM ** -0.5
    outs = []
    for h in range(MEM_HEADS):
        sl = slice(h * MEM_HEAD_DIM, (h + 1) * MEM_HEAD_DIM)
        kh = kv_ref[0, :, sl]
        vh = kv_ref[0, :, MEM_WIDTH + h * MEM_HEAD_DIM:MEM_WIDTH + (h + 1) * MEM_HEAD_DIM]
        s = _dot_nt(q[:, sl], kh) * scale
        m = jnp.max(s, axis=-1, keepdims=True)
        e = jnp.exp(s - m)
        p = (e / jnp.sum(e, axis=-1, keepdims=True)).astype(BF16)
        outs.append(jnp.dot(p, vh, preferred_element_type=F32).astype(BF16))
    o = jnp.concatenate(outs, axis=-1)
    y_ref[...] = x + jnp.dot(o, wo_ref[...], preferred_element_type=F32)


def cross_attention(x, meta, g, wq, kv, wo, *, bm):
    t, d = x.shape
    grid_spec = pltpu.PrefetchScalarGridSpec(
        num_scalar_prefetch=1,
        grid=(t // bm,),
        in_specs=[
            pl.BlockSpec((bm, d), lambda i, m: (i, 0)),
            pl.BlockSpec((1, d), lambda i, m: (0, 0)),
            pl.BlockSpec((d, MEM_WIDTH), lambda i, m: (0, 0)),
            pl.BlockSpec((1, MEM_TOKENS, 2 * MEM_WIDTH), lambda i, m: (m[3, i], 0, 0)),
            pl.BlockSpec((MEM_WIDTH, d), lambda i, m: (0, 0)),
        ],
        out_specs=pl.BlockSpec((bm, d), lambda i, m: (i, 0)),
    )
    return pl.pallas_call(
        _cross_kernel,
        grid_spec=grid_spec,
        out_shape=jax.ShapeDtypeStruct((t, d), F32),
        compiler_params=_cparams("parallel"),
        name="cross_attention",
    )(meta, x, g.reshape(1, d), wq, kv, wo)


N_TOP = PEER_TOPK + 1
CAND = [(a, b) for a in range(N_TOP) for b in range(N_TOP // (a + 1))]
CAND_ROWS = 8 * SUBLANES
assert len(CAND) <= CAND_ROWS
XPOSE_COLS = 256


def _oddeven_merge_sort(n):
    def merge(lo, hi, r):
        step = r * 2
        if step < hi - lo:
            yield from merge(lo, hi, step)
            yield from merge(lo + r, hi, step)
            yield from [(i, i + r) for i in range(lo + r, hi - r, step)]
        else:
            yield (lo, lo + r)

    def sort(lo, hi):
        if hi - lo >= 1:
            mid = lo + (hi - lo) // 2
            yield from sort(lo, mid)
            yield from sort(mid + 1, hi)
            yield from merge(lo, hi, 1)

    return list(sort(0, n - 1))


def _top_values(s, n_top, one_at_a_time=False):
    nblk = s.shape[0] // SUBLANES
    v = [s[d * SUBLANES:(d + 1) * SUBLANES, :] for d in range(nblk)]
    for a, b in _oddeven_merge_sort(nblk):
        v[a], v[b] = jnp.maximum(v[a], v[b]), jnp.minimum(v[a], v[b])
    sub = lax.broadcasted_iota(jnp.int32, v[0].shape, 0)
    vals = []
    for k in range(n_top):
        m = jnp.max(v[0], axis=0, keepdims=True)
        vals.append(m)
        remaining = n_top - 1 - k
        if remaining == 0:
            break
        hit = v[0] == m
        if one_at_a_time:
            first = jnp.min(jnp.where(hit, sub, SUBLANES), axis=0, keepdims=True)
            hit = sub == first
        for d in range(min(nblk, remaining)):
            v[d] = jnp.where(hit, v[d + 1] if d + 1 < nblk else NEG_INF, v[d])
    return vals


def _count_above(vals, x, strict):
    assert len(vals) == 16

    def above(v):
        return (v > x) if strict else (v >= x)

    bits = []

    def pivot(level, lo, size, depth=0):
        mid = lo + size // 2
        if depth == level:
            return vals[mid]
        return jnp.where(bits[depth], pivot(level, mid + 1, size // 2, depth + 1),
                         pivot(level, lo, size // 2, depth + 1))

    count = jnp.where(above(vals[15]), 1.0, 0.0)
    for level in range(4):
        bits.append(above(pivot(level, 0, 15)))
        count = count + jnp.where(bits[level], float(8 >> level), 0.0)
    return count


def _router_kernel(x_ref, g_ref, wqt_ref, k1_ref, k2_ref, xnt_ref, r2_ref, bz_ref, cnt_ref, a_ref,
                   qt_scr, c_scr):
    d = x_ref.shape[1]
    x = x_ref[...]
    inv = lax.rsqrt(jnp.mean(x * x, axis=-1, keepdims=True) + NORM_EPS)
    for c0 in range(0, d, XPOSE_COLS):
        cols = slice(c0, c0 + XPOSE_COLS)
        xn_c = x_ref[:, cols] * inv * g_ref[:, cols]
        xnt_ref[cols, :] = xn_c.T.astype(BF16)
    qt_scr[...] = jnp.dot(wqt_ref[...], xnt_ref[...],
                          preferred_element_type=F32).astype(BF16)
    c_scr[...] = jnp.full(c_scr.shape, NEG_INF, F32)

    def head(h, carry):
        off = pl.multiple_of(h * PEER_QDIM, PEER_QDIM)
        s1 = jnp.dot(k1_ref[...], qt_scr[pl.ds(off, PEER_HALF), :], preferred_element_type=F32)
        s2 = jnp.dot(k2_ref[...], qt_scr[pl.ds(off + PEER_HALF, PEER_HALF), :],
                     preferred_element_type=F32)
        v1 = _top_values(s1, N_TOP)
        v2 = _top_values(s2, N_TOP)
        for r, (a, b) in enumerate(CAND):
            c_scr[r:r + 1, :] = v1[a] + v2[b]
        c0 = c_scr[...]
        csort = _top_values(c0, N_TOP, one_at_a_time=True)
        thr = 0.5 * (csort[PEER_TOPK - 1] + csort[PEER_TOPK])
        top = v1[0] + v2[0]
        z = jnp.sum(jnp.where(c0 >= thr, jnp.exp(c0 - top), 0.0), axis=0, keepdims=True)
        r2_ref[h] = _count_above(v2[:PEER_TOPK], s2, strict=True).astype(BF16)
        bz_ref[h] = (jnp.exp(s2 - v2[0]) / z).astype(BF16)
        cnt_ref[h] = _count_above(v2[:PEER_TOPK], thr - s1, strict=False)
        a_ref[h] = jnp.exp(s1 - v1[0])
        return carry

    lax.fori_loop(0, PEER_HEADS, head, 0)


def peer_router(x, g, wqt, k1, k2, *, bm):
    t, d = x.shape
    head_spec = pl.BlockSpec((PEER_HEADS, N_KEYS, bm), lambda i: (0, 0, i))
    return pl.pallas_call(
        _router_kernel,
        grid=(t // bm,),
        in_specs=[
            pl.BlockSpec((bm, d), lambda i: (i, 0)),
            pl.BlockSpec((1, d), lambda i: (0, 0)),
            pl.BlockSpec((PEER_HEADS * PEER_QDIM, d), lambda i: (0, 0)),
            pl.BlockSpec((N_KEYS, PEER_HALF), lambda i: (0, 0)),
            pl.BlockSpec((N_KEYS, PEER_HALF), lambda i: (0, 0)),
        ],
        out_specs=[pl.BlockSpec((d, bm), lambda i: (0, i)), head_spec, head_spec, head_spec, head_spec],
        out_shape=[jax.ShapeDtypeStruct((d, t), BF16),
                   jax.ShapeDtypeStruct((PEER_HEADS, N_KEYS, t), BF16),
                   jax.ShapeDtypeStruct((PEER_HEADS, N_KEYS, t), BF16),
                   jax.ShapeDtypeStruct((PEER_HEADS, N_KEYS, t), F32),
                   jax.ShapeDtypeStruct((PEER_HEADS, N_KEYS, t), F32)],
        scratch_shapes=[pltpu.VMEM((PEER_HEADS * PEER_QDIM, bm), BF16), pltpu.VMEM((CAND_ROWS, bm), F32)],
        compiler_params=_cparams("parallel"),
        name="peer_router",
    )(x, g.reshape(1, d), wqt, k1, k2)


EXPERT_TILE = 2048
EXPERT_VMEM_LIMIT = 62 * 1024 * 1024
EXPERT_CHUNK = 256
KEYS_PER_TILE = EXPERT_TILE // N_KEYS
KEYS_PER_CHUNK = EXPERT_CHUNK // N_KEYS
CHUNKS_PER_TILE = EXPERT_TILE // EXPERT_CHUNK
TOKEN_GROUP = 2 * LANES
assert KEYS_PER_TILE % SUBLANES == 0


def _expert_kernel(xnt_ref, u_ref, vt_ref, r2_ref, bz_ref, cnt_ref, a_ref, x_hbm, gf_ref, y_ref, p_scr,
                   acc_scr, x_sem, *, tile_off, final_norm):
    e = pl.program_id(1)
    bm = y_ref.shape[0]

    def x_copy():
        row0 = pl.multiple_of((pl.program_id(0) + tile_off) * bm, bm)
        return pltpu.make_async_copy(x_hbm.at[pl.ds(row0, bm), :], y_ref, x_sem)

    @pl.when(e == 0)
    def _():
        acc_scr[...] = jnp.zeros(acc_scr.shape, F32)
        x_copy().start()

    for c in range(CHUNKS_PER_TILE):
        crow = slice(c * EXPERT_CHUNK, (c + 1) * EXPERT_CHUNK)
        h = jnp.dot(u_ref[crow, :], xnt_ref[...], preferred_element_type=F32)
        for kk in range(KEYS_PER_CHUNK):
            il = c * KEYS_PER_CHUNK + kk
            rows = slice(il * N_KEYS, (il + 1) * N_KEYS)
            for tg in range(bm // TOKEN_GROUP):
                ts = slice(tg * TOKEN_GROUP, (tg + 1) * TOKEN_GROUP)
                g = jax.nn.gelu(h[kk * N_KEYS:(kk + 1) * N_KEYS, ts].astype(BF16))
                p = jnp.zeros((N_KEYS, TOKEN_GROUP), BF16)
                for hd in range(PEER_HEADS):
                    cb = jnp.broadcast_to(cnt_ref[hd, il:il + 1, ts], (N_KEYS, TOKEN_GROUP)).astype(BF16)
                    ab = jnp.broadcast_to(a_ref[hd, il:il + 1, ts], (N_KEYS, TOKEN_GROUP)).astype(BF16)
                    p = p + jnp.where(r2_ref[hd, :, ts] < cb, bz_ref[hd, :, ts], 0.0) * ab
                p_scr[rows, ts] = p * g
    acc_scr[...] += jnp.dot(vt_ref[...], p_scr[...], preferred_element_type=F32)

    @pl.when(e == pl.num_programs(1) - 1)
    def _():
        x_copy().wait()
        for c0 in range(0, y_ref.shape[1], XPOSE_COLS):
            cols = slice(c0, c0 + XPOSE_COLS)
            y_ref[:, cols] += acc_scr[cols, :].T
        if final_norm:
            y_ref[...] = _rms(y_ref[...], gf_ref[...])


def peer_experts(xnt, u, vt, r2, bz, cnt, a, x, g_final, *, bm, start=0, rows=None, final_norm=False):
    d = x.shape[1]
    rows = x.shape[0] if rows is None else rows
    assert bm % TOKEN_GROUP == 0 and start % bm == 0 and rows % bm == 0
    off = start // bm
    head_full = pl.BlockSpec((PEER_HEADS, N_KEYS, bm), lambda i, e: (0, 0, i + off))
    head_rows = pl.BlockSpec((PEER_HEADS, KEYS_PER_TILE, bm), lambda i, e: (0, e, i + off))
    return pl.pallas_call(
        functools.partial(_expert_kernel, tile_off=off, final_norm=final_norm),
        grid=(rows // bm, u.shape[0] // EXPERT_TILE),
        in_specs=[
            pl.BlockSpec((d, bm), lambda i, e: (0, i + off)),
            pl.BlockSpec((EXPERT_TILE, d), lambda i, e: (e, 0)),
            pl.BlockSpec((d, EXPERT_TILE), lambda i, e: (0, e)),
            head_full, head_full, head_rows, head_rows,
            pl.BlockSpec(memory_space=pl.ANY),
            pl.BlockSpec((1, d), lambda i, e: (0, 0)),
        ],
        out_specs=pl.BlockSpec((bm, d), lambda i, e: (i, 0)),
        out_shape=jax.ShapeDtypeStruct((rows, d), F32),
        scratch_shapes=[
            pltpu.VMEM((EXPERT_TILE, bm), BF16),
            pltpu.VMEM((d, bm), F32),
            pltpu.SemaphoreType.DMA(()),
        ],
        compiler_params=pltpu.CompilerParams(dimension_semantics=("parallel", "arbitrary"),
                                             vmem_limit_bytes=EXPERT_VMEM_LIMIT),
        name="peer_experts",
    )(xnt, u, vt, r2, bz, cnt, a, x, g_final.reshape(1, d))


def _block_meta(groups, blk):
    pos, first, last, batch = [], [], [], []
    b0 = 0
    for (b, s) in groups:
        per = s // blk
        for bi in range(b):
            for j in range(per):
                pos.append(j)
                first.append(int(j == 0))
                last.append(int(j == per - 1))
                batch.append(b0 + bi)
        b0 += b
    return jnp.asarray(np.array([pos, first, last, batch], dtype=np.int32))


def _rope_table(max_pos):
    half = HEAD_DIM // 2
    inv = ROPE_THETA ** (-jnp.arange(half, dtype=F32) / half)
    ang = jnp.arange(max_pos, dtype=F32)[:, None] * inv[None, :]
    cos, sin = jnp.cos(ang), jnp.sin(ang)
    return jnp.concatenate([cos, cos, -sin, sin], axis=-1)


def _pick(t, pref):
    while t % pref:
        pref //= 2
    return pref


def _trunk(xs, mems, mix_norm, w_in, attn_sink, gm_ln_g, gm_ln_b, gm_w_s, gm_b_s, w_att_proj, w_gm_proj,
           w_out, cross_norm, mem_norm, w_q_mem, w_kv_mem, w_o_mem, peer_norm, w_q_peer, peer_k1, peer_k2,
           expert_u, expert_v, final_norm):
    groups = [(x.shape[0], x.shape[1]) for x in xs]
    depth = w_in.shape[0]
    for (_, s) in groups:
        assert s % ATT_BLOCK == 0
    x_parts = tuple(x.reshape(-1, D_MODEL) for x in xs)
    if len(x_parts) > 2:
        x_parts = (jnp.concatenate(x_parts, axis=0),)
    mem = jnp.concatenate([m.reshape(-1, D_MODEL) for m in mems], axis=0)
    n_mem_rows = mem.shape[0]
    meta = _block_meta(groups, ATT_BLOCK)
    cs_table = _rope_table(max(s for _, s in groups))

    common = int(np.gcd.reduce([b * s for b, s in groups]))
    bm_big = _pick(common, 1024)
    bm_mid = _pick(common, 512)
    bm_small = _pick(common, 256)
    in_splits = np.cumsum([ATT_WIDTH, KV_WIDTH, KV_WIDTH, GM_WIDTH, GM_WIDTH, D_MODEL]).tolist()

    for l in range(depth):
        q_w, k_w, v_w, gu_w, gz_w, ga_w, gb_w = jnp.split(w_in[l], in_splits, axis=1)
        w_in_p = jnp.concatenate([q_w, gu_w, gz_w, ga_w, gb_w, k_w, v_w], axis=1).astype(BF16)
        sink_col = jnp.broadcast_to((attn_sink[l].astype(F32) * LOG2_E)[:, None, None],
                                    (N_Q_HEADS, WINDOW, SINK_LANES))
        b_s_b = jnp.broadcast_to(gm_b_s[l].astype(F32)[:, :, None],
                                 (GM_GROUPS, GM_CHUNK, GM_WIDTH // GM_GROUPS))

        proj = norm_matmul(x_parts, mix_norm[l], w_in_p, bm=bm_big, bn=1024)
        att = banded_attention(proj, meta, cs_table, sink_col)
        gm = spatial_gating(proj, gm_ln_g[l], gm_ln_b[l], gm_w_s[l].astype(BF16), b_s_b, bm=ATT_BLOCK)
        x = merge_project(att, gm, proj, x_parts, w_att_proj[l].astype(BF16), w_gm_proj[l].astype(BF16),
                          w_out[l].astype(BF16), bm=bm_small)

        kv = norm_matmul((mem,), mem_norm[l], w_kv_mem[l].astype(BF16), bm=_pick(n_mem_rows, 512), bn=1024)
        kv = kv.reshape(-1, MEM_TOKENS, 2 * MEM_WIDTH)
        x = cross_attention(x, meta, cross_norm[l], w_q_mem[l].astype(BF16), kv, w_o_mem[l].astype(BF16),
                            bm=ATT_BLOCK)

        xnt, r2, bz, cnt, a = peer_router(x, peer_norm[l], w_q_peer[l].T.astype(BF16),
                                          peer_k1[l].astype(BF16), peer_k2[l].astype(BF16), bm=bm_mid)
        experts = functools.partial(peer_experts, xnt, expert_u[l].astype(BF16), expert_v[l].T.astype(BF16),
                                    r2, bz, cnt, a, x, final_norm, bm=bm_mid)
        if l + 1 < depth:
            x_parts = (experts(),)
    outs = []
    start = 0
    for (b, s) in groups:
        y = experts(start=start, rows=b * s, final_norm=True)
        outs.append(y.reshape(b, s, D_MODEL))
        start += b * s
    return tuple(outs)


def kernel(x_prompt, x_sample, mem_prompt, mem_sample, mix_norm, w_in, attn_sink, gm_ln_g, gm_ln_b, gm_w_s, gm_b_s, w_att_proj, w_gm_proj, w_out, cross_norm, mem_norm, w_q_mem, w_kv_mem, w_o_mem, peer_norm, w_q_peer, peer_k1, peer_k2, expert_u, expert_v, final_norm):
    return _trunk((x_prompt, x_sample), (mem_prompt, mem_sample), mix_norm, w_in, attn_sink, gm_ln_g,
                  gm_ln_b, gm_w_s, gm_b_s, w_att_proj, w_gm_proj, w_out, cross_norm, mem_norm, w_q_mem,
                  w_kv_mem, w_o_mem, peer_norm, w_q_peer, peer_k1, peer_k2, expert_u, expert_v, final_norm)
```

```python
import functools

import numpy as np
import jax
import jax.numpy as jnp
from jax import lax
from jax.experimental import pallas as pl
from jax.experimental.pallas import tpu as pltpu

F32 = jnp.float32
BF16 = jnp.bfloat16

D_MODEL = 2048
HEAD_DIM = 128
N_Q_HEADS = 16
N_KV_HEADS = 4
Q_PER_KV = N_Q_HEADS // N_KV_HEADS
ATT_WIDTH = N_Q_HEADS * HEAD_DIM
KV_WIDTH = N_KV_HEADS * HEAD_DIM
WINDOW = 128
ROPE_THETA = 10000.0
GM_WIDTH = 2048
GM_GROUPS = 16
GM_CHUNK = 128
MEM_TOKENS = 256
MEM_HEADS = 4
MEM_HEAD_DIM = 128
MEM_WIDTH = MEM_HEADS * MEM_HEAD_DIM
PEER_HEADS = 8
PEER_QDIM = 256
PEER_HALF = PEER_QDIM // 2
N_KEYS = 128
N_EXPERTS = N_KEYS * N_KEYS
PEER_TOPK = 16
NORM_EPS = 1e-6
NEG_INF = -1e30
LOG2_E = 1.4426950408889634

COL_Q, COL_GU, COL_GZ, COL_GA, COL_GB = 0, 1, 2, 3, 4
COL_K = (ATT_WIDTH + 4 * D_MODEL) // KV_WIDTH
COL_V = COL_K + 1

LANES = 128
SUBLANES = 8
ATT_BLOCK = 512
SINK_LANES = LANES
HEADS_PER_PASS = 2
VMEM_LIMIT = 56 * 1024 * 1024


def _cparams(*sem):
    return pltpu.CompilerParams(dimension_semantics=sem, vmem_limit_bytes=VMEM_LIMIT)


def _rms(x, g):
    return x * lax.rsqrt(jnp.mean(x * x, axis=-1, keepdims=True) + NORM_EPS) * g


def _dot_nt(a, b):
    return lax.dot_general(a, b, (((1,), (1,)), ((), ())), preferred_element_type=F32)


def _row_parts(xs, bm):
    xa, xb = (xs[0], xs[0]) if len(xs) == 1 else xs
    assert all(x.shape[0] % bm == 0 for x in xs)
    return xa, xb, xa.shape[0] // bm, sum(x.shape[0] for x in xs)


def _part_specs(bm, d, na, ndim_grid):
    if ndim_grid == 1:
        return [pl.BlockSpec((bm, d), lambda i: (jnp.minimum(i, na - 1), 0)),
                pl.BlockSpec((bm, d), lambda i: (jnp.maximum(i - na, 0), 0))]
    return [pl.BlockSpec((bm, d), lambda i, j: (jnp.minimum(i, na - 1), 0)),
            pl.BlockSpec((bm, d), lambda i, j: (jnp.maximum(i - na, 0), 0))]


def _norm_matmul_kernel(xa_ref, xb_ref, g_ref, w_ref, o_ref, xn_ref, *, na):
    i = pl.program_id(0)
    first = pl.program_id(1) == 0

    @pl.when(jnp.logical_and(first, i < na))
    def _():
        xn_ref[...] = _rms(xa_ref[...], g_ref[...]).astype(BF16)

    @pl.when(jnp.logical_and(first, i >= na))
    def _():
        xn_ref[...] = _rms(xb_ref[...], g_ref[...]).astype(BF16)

    o_ref[...] = jnp.dot(xn_ref[...], w_ref[...], preferred_element_type=F32).astype(o_ref.dtype)


def norm_matmul(xs, g, w, *, bm, bn):
    xa, xb, na, t = _row_parts(xs, bm)
    d = xa.shape[1]
    n = w.shape[1]
    assert n % bn == 0
    return pl.pallas_call(
        functools.partial(_norm_matmul_kernel, na=na),
        grid=(t // bm, n // bn),
        in_specs=_part_specs(bm, d, na, 2) + [
            pl.BlockSpec((1, d), lambda i, j: (0, 0)),
            pl.BlockSpec((d, bn), lambda i, j: (0, j)),
        ],
        out_specs=pl.BlockSpec((bm, bn), lambda i, j: (i, j)),
        out_shape=jax.ShapeDtypeStruct((t, n), BF16),
        scratch_shapes=[pltpu.VMEM((bm, d), BF16)],
        compiler_params=_cparams("parallel", "arbitrary"),
        name="norm_matmul",
    )(xa, xb, g.reshape(1, d), w)


def _attn_kernel(meta_ref, q_ref, kc_ref, kp_ref, kn_ref, vc_ref, vp_ref, vn_ref,
                 csc_ref, csp_ref, csn_ref, sink_ref, o_ref, q_scr, k_scr, v_scr, o_scr):
    n = pl.program_id(0)
    nb = ATT_BLOCK // WINDOW
    band = 3 * WINDOW
    k_lo = jnp.where(meta_ref[1, n] == 1, WINDOW, 0)
    k_hi = jnp.where(meta_ref[2, n] == 1, 2 * WINDOW, band)

    def rope(x, cs):
        return x * cs[:, :HEAD_DIM] + pltpu.roll(x, HEAD_DIM // 2, 1) * cs[:, HEAD_DIM:]

    cs_c = csc_ref[...]
    for hq in range(N_Q_HEADS):
        sl = slice(hq * HEAD_DIM, (hq + 1) * HEAD_DIM)
        q_scr[hq] = rope(q_ref[:, sl].astype(F32), cs_c).astype(BF16)
    cs_p = csp_ref[...]
    cs_n = csn_ref[...]
    for h in range(N_KV_HEADS):
        sl = slice(h * HEAD_DIM, (h + 1) * HEAD_DIM)
        k_scr[h, 0:WINDOW, :] = rope(kp_ref[:, sl].astype(F32), cs_p).astype(BF16)
        k_scr[h, WINDOW:WINDOW + ATT_BLOCK, :] = rope(kc_ref[:, sl].astype(F32), cs_c).astype(BF16)
        k_scr[h, WINDOW + ATT_BLOCK:, :] = rope(kn_ref[:, sl].astype(F32), cs_n).astype(BF16)
        v_scr[h, 0:WINDOW, :] = vp_ref[:, sl]
        v_scr[h, WINDOW:WINDOW + ATT_BLOCK, :] = vc_ref[:, sl]
        v_scr[h, WINDOW + ATT_BLOCK:, :] = vn_ref[:, sl]

    qi = lax.broadcasted_iota(jnp.int32, (WINDOW, band), 0)
    kc = lax.broadcasted_iota(jnp.int32, (WINDOW, band), 1)
    rel = jnp.abs(qi + WINDOW - kc) <= WINDOW
    scale = HEAD_DIM ** -0.5 * LOG2_E

    def band_step(r, carry):
        r0 = pl.multiple_of(r * WINDOW, WINDOW)
        lo = jnp.where(r == 0, k_lo, 0)
        hi = jnp.where(r == nb - 1, k_hi, band)
        bias = jnp.where(jnp.logical_and(rel, jnp.logical_and(kc >= lo, kc < hi)), 0.0, NEG_INF)
        for h0 in range(0, N_KV_HEADS, HEADS_PER_PASS):
            heads = range(h0, h0 + HEADS_PER_PASS)
            kbs = {h: k_scr[h, pl.ds(r0, band), :] for h in heads}
            vbs = {h: v_scr[h, pl.ds(r0, band), :] for h in heads}
            scores = {(h, g): _dot_nt(q_scr[Q_PER_KV * h + g, pl.ds(r0, WINDOW), :], kbs[h])
                      for h in heads for g in range(Q_PER_KV)}
            for h in heads:
                for g in range(Q_PER_KV):
                    hq = Q_PER_KV * h + g
                    s = scores[(h, g)] * scale + bias
                    sk = sink_ref[hq]
                    m = jnp.maximum(jnp.max(s, axis=-1, keepdims=True), sk)
                    e = jnp.concatenate([jnp.exp2(s[:, j * WINDOW:(j + 1) * WINDOW] - m) for j in range(3)],
                                        axis=-1)
                    den = jnp.sum(e, axis=-1, keepdims=True) + jnp.exp2(sk - m)
                    o = jnp.dot(e.astype(BF16), vbs[h], preferred_element_type=F32) * (1.0 / den)
                    o_scr[hq, pl.ds(r0, WINDOW), :] = o.astype(BF16)
        return carry

    lax.fori_loop(0, nb, band_step, 0)
    for hq in range(N_Q_HEADS):
        o_ref[:, hq * HEAD_DIM:(hq + 1) * HEAD_DIM] = o_scr[hq]


def banded_attention(proj, meta, cs_table, sink_col):
    t = proj.shape[0]
    nblk = t // ATT_BLOCK
    sub = ATT_BLOCK // WINDOW
    n_small = t // WINDOW
    n_pos_small = cs_table.shape[0] // WINDOW
    grid_spec = pltpu.PrefetchScalarGridSpec(
        num_scalar_prefetch=1,
        grid=(nblk,),
        in_specs=[
            pl.BlockSpec((ATT_BLOCK, ATT_WIDTH), lambda n, m: (n, COL_Q)),
            pl.BlockSpec((ATT_BLOCK, KV_WIDTH), lambda n, m: (n, COL_K)),
            pl.BlockSpec((WINDOW, KV_WIDTH), lambda n, m: (jnp.maximum(n * sub - 1, 0), COL_K)),
            pl.BlockSpec((WINDOW, KV_WIDTH), lambda n, m: (jnp.minimum(n * sub + sub, n_small - 1), COL_K)),
            pl.BlockSpec((ATT_BLOCK, KV_WIDTH), lambda n, m: (n, COL_V)),
            pl.BlockSpec((WINDOW, KV_WIDTH), lambda n, m: (jnp.maximum(n * sub - 1, 0), COL_V)),
            pl.BlockSpec((WINDOW, KV_WIDTH), lambda n, m: (jnp.minimum(n * sub + sub, n_small - 1), COL_V)),
            pl.BlockSpec((ATT_BLOCK, 2 * HEAD_DIM), lambda n, m: (m[0, n], 0)),
            pl.BlockSpec((WINDOW, 2 * HEAD_DIM), lambda n, m: (jnp.maximum(m[0, n] * sub - 1, 0), 0)),
            pl.BlockSpec((WINDOW, 2 * HEAD_DIM),
                         lambda n, m: (jnp.minimum(m[0, n] * sub + sub, n_pos_small - 1), 0)),
            pl.BlockSpec((N_Q_HEADS, WINDOW, SINK_LANES), lambda n, m: (0, 0, 0)),
        ],
        out_specs=pl.BlockSpec((ATT_BLOCK, ATT_WIDTH), lambda n, m: (n, 0)),
        scratch_shapes=[
            pltpu.VMEM((N_Q_HEADS, ATT_BLOCK, HEAD_DIM), BF16),
            pltpu.VMEM((N_KV_HEADS, ATT_BLOCK + 2 * WINDOW, HEAD_DIM), BF16),
            pltpu.VMEM((N_KV_HEADS, ATT_BLOCK + 2 * WINDOW, HEAD_DIM), BF16),
            pltpu.VMEM((N_Q_HEADS, ATT_BLOCK, HEAD_DIM), BF16),
        ],
    )
    return pl.pallas_call(
        _attn_kernel,
        grid_spec=grid_spec,
        out_shape=jax.ShapeDtypeStruct((t, ATT_WIDTH), BF16),
        compiler_params=_cparams("parallel"),
        name="banded_attention",
    )(meta, proj, proj, proj, proj, proj, proj, proj, cs_table, cs_table, cs_table, sink_col)


def _gmlp_kernel(gu_ref, gz_ref, lng_ref, lnb_ref, ws_ref, bs_ref, o_ref, zn_scr, u_scr):
    z = jax.nn.gelu(gz_ref[...].astype(F32))
    mu = jnp.mean(z, axis=-1, keepdims=True)
    zc = z - mu
    var = jnp.mean(zc * zc, axis=-1, keepdims=True)
    zn_scr[...] = (zc * lax.rsqrt(var + NORM_EPS) * lng_ref[...] + lnb_ref[...]).astype(BF16)
    u_scr[...] = jax.nn.gelu(gu_ref[...])
    gd = GM_WIDTH // GM_GROUPS
    for c in range(o_ref.shape[0] // GM_CHUNK):
        rs = slice(c * GM_CHUNK, (c + 1) * GM_CHUNK)
        for g in range(GM_GROUPS):
            cs = slice(g * gd, (g + 1) * gd)
            mixed = jnp.dot(ws_ref[g], zn_scr[rs, cs], preferred_element_type=F32) + bs_ref[g]
            o_ref[rs, cs] = (u_scr[rs, cs].astype(F32) * mixed).astype(BF16)


def spatial_gating(proj, ln_g, ln_b, w_s, b_s_b, *, bm):
    t = proj.shape[0]
    return pl.pallas_call(
        _gmlp_kernel,
        grid=(t // bm,),
        in_specs=[
            pl.BlockSpec((bm, GM_WIDTH), lambda i: (i, COL_GU)),
            pl.BlockSpec((bm, GM_WIDTH), lambda i: (i, COL_GZ)),
            pl.BlockSpec((1, GM_WIDTH), lambda i: (0, 0)),
            pl.BlockSpec((1, GM_WIDTH), lambda i: (0, 0)),
            pl.BlockSpec((GM_GROUPS, GM_CHUNK, GM_CHUNK), lambda i: (0, 0, 0)),
            pl.BlockSpec((GM_GROUPS, GM_CHUNK, GM_WIDTH // GM_GROUPS), lambda i: (0, 0, 0)),
        ],
        out_specs=pl.BlockSpec((bm, GM_WIDTH), lambda i: (i, 0)),
        out_shape=jax.ShapeDtypeStruct((t, GM_WIDTH), BF16),
        scratch_shapes=[pltpu.VMEM((bm, GM_WIDTH), BF16), pltpu.VMEM((bm, GM_WIDTH), BF16)],
        compiler_params=_cparams("parallel"),
        name="spatial_gating",
    )(proj, proj, ln_g.reshape(1, -1), ln_b.reshape(1, -1), w_s, b_s_b)


def _sigmoid(x):
    return 1.0 / (1.0 + jnp.exp(-x))


def _merge_kernel(att_ref, gm_ref, ga_ref, gb_ref, xa_ref, xb_ref, wa_ref, wg_ref, wo_ref, y_ref, *, na):
    i = pl.program_id(0)

    @pl.when(i < na)
    def _():
        y_ref[...] = xa_ref[...]

    @pl.when(i >= na)
    def _():
        y_ref[...] = xb_ref[...]

    a = jnp.dot(att_ref[...], wa_ref[...], preferred_element_type=F32)
    b = jnp.dot(gm_ref[...], wg_ref[...], preferred_element_type=F32)
    merged = _sigmoid(ga_ref[...].astype(F32)) * a + _sigmoid(gb_ref[...].astype(F32)) * b
    y_ref[...] += jnp.dot(merged.astype(BF16), wo_ref[...], preferred_element_type=F32)


def merge_project(att, gm, proj, xs, wa, wg, wo, *, bm):
    xa, xb, na, t = _row_parts(xs, bm)
    d = xa.shape[1]
    const = dict(pipeline_mode=pl.Buffered(1))
    return pl.pallas_call(
        functools.partial(_merge_kernel, na=na),
        grid=(t // bm,),
        in_specs=[
            pl.BlockSpec((bm, ATT_WIDTH), lambda i: (i, 0)),
            pl.BlockSpec((bm, GM_WIDTH), lambda i: (i, 0)),
            pl.BlockSpec((bm, d), lambda i: (i, COL_GA)),
            pl.BlockSpec((bm, d), lambda i: (i, COL_GB)),
        ] + _part_specs(bm, d, na, 1) + [
            pl.BlockSpec((ATT_WIDTH, d), lambda i: (0, 0), **const),
            pl.BlockSpec((GM_WIDTH, d), lambda i: (0, 0), **const),
            pl.BlockSpec((d, d), lambda i: (0, 0), **const),
        ],
        out_specs=pl.BlockSpec((bm, d), lambda i: (i, 0)),
        out_shape=jax.ShapeDtypeStruct((t, d), F32),
        compiler_params=_cparams("parallel"),
        name="merge_project",
    )(att, gm, proj, proj, xa, xb, wa, wg, wo)


def _cross_kernel(meta_ref, x_ref, g_ref, wq_ref, kv_ref, wo_ref, y_ref):
    x = x_ref[...]
    xn = _rms(x, g_ref[...]).astype(BF16)
    q = jnp.dot(xn, wq_ref[...], preferred_element_type=F32).astype(BF16)
    scale = MEM_HEAD_DIM ** -0.5
    outs = []
    for h in range(MEM_HEADS):
        sl = slice(h * MEM_HEAD_DIM, (h + 1) * MEM_HEAD_DIM)
        kh = kv_ref[0, :, sl]
        vh = kv_ref[0, :, MEM_WIDTH + h * MEM_HEAD_DIM:MEM_WIDTH + (h + 1) * MEM_HEAD_DIM]
        s = _dot_nt(q[:, sl], kh) * scale
        m = jnp.max(s, axis=-1, keepdims=True)
        e = jnp.exp(s - m)
        p = (e / jnp.sum(e, axis=-1, keepdims=True)).astype(BF16)
        outs.append(jnp.dot(p, vh, preferred_element_type=F32).astype(BF16))
    o = jnp.concatenate(outs, axis=-1)
    y_ref[...] = x + jnp.dot(o, wo_ref[...], preferred_element_type=F32)


def cross_attention(x, meta, g, wq, kv, wo, *, bm):
    t, d = x.shape
    grid_spec = pltpu.PrefetchScalarGridSpec(
        num_scalar_prefetch=1,
        grid=(t // bm,),
        in_specs=[
            pl.BlockSpec((bm, d), lambda i, m: (i, 0)),
            pl.BlockSpec((1, d), lambda i, m: (0, 0)),
            pl.BlockSpec((d, MEM_WIDTH), lambda i, m: (0, 0)),
            pl.BlockSpec((1, MEM_TOKENS, 2 * MEM_WIDTH), lambda i, m: (m[3, i], 0, 0)),
            pl.BlockSpec((MEM_WIDTH, d), lambda i, m: (0, 0)),
        ],
        out_specs=pl.BlockSpec((bm, d), lambda i, m: (i, 0)),
    )
    return pl.pallas_call(
        _cross_kernel,
        grid_spec=grid_spec,
        out_shape=jax.ShapeDtypeStruct((t, d), F32),
        compiler_params=_cparams("parallel"),
        name="cross_attention",
    )(meta, x, g.reshape(1, d), wq, kv, wo)


N_TOP = PEER_TOPK + 1
CAND = [(a, b) for a in range(N_TOP) for b in range(N_TOP // (a + 1))]
CAND_ROWS = 8 * SUBLANES
assert len(CAND) <= CAND_ROWS
XPOSE_COLS = 256


def _oddeven_merge_sort(n):
    def merge(lo, hi, r):
        step = r * 2
        if step < hi - lo:
            yield from merge(lo, hi, step)
            yield from merge(lo + r, hi, step)
            yield from [(i, i + r) for i in range(lo + r, hi - r, step)]
        else:
            yield (lo, lo + r)

    def sort(lo, hi):
        if hi - lo >= 1:
            mid = lo + (hi - lo) // 2
            yield from sort(lo, mid)
            yield from sort(mid + 1, hi)
            yield from merge(lo, hi, 1)

    return list(sort(0, n - 1))


def _top_values(s, n_top, one_at_a_time=False):
    nblk = s.shape[0] // SUBLANES
    v = [s[d * SUBLANES:(d + 1) * SUBLANES, :] for d in range(nblk)]
    for a, b in _oddeven_merge_sort(nblk):
        v[a], v[b] = jnp.maximum(v[a], v[b]), jnp.minimum(v[a], v[b])
    sub = lax.broadcasted_iota(jnp.int32, v[0].shape, 0)
    vals = []
    for k in range(n_top):
        m = jnp.max(v[0], axis=0, keepdims=True)
        vals.append(m)
        remaining = n_top - 1 - k
        if remaining == 0:
            break
        hit = v[0] == m
        if one_at_a_time:
            first = jnp.min(jnp.where(hit, sub, SUBLANES), axis=0, keepdims=True)
            hit = sub == first
        for d in range(min(nblk, remaining)):
            v[d] = jnp.where(hit, v[d + 1] if d + 1 < nblk else NEG_INF, v[d])
    return vals


def _count_above(vals, x, strict):
    assert len(vals) == 16

    def above(v):
        return (v > x) if strict else (v >= x)

    bits = []

    def pivot(level, lo, size, depth=0):
        mid = lo + size // 2
        if depth == level:
            return vals[mid]
        return jnp.where(bits[depth], pivot(level, mid + 1, size // 2, depth + 1),
                         pivot(level, lo, size // 2, depth + 1))

    count = jnp.where(above(vals[15]), 1.0, 0.0)
    for level in range(4):
        bits.append(above(pivot(level, 0, 15)))
        count = count + jnp.where(bits[level], float(8 >> level), 0.0)
    return count


def _router_kernel(x_ref, g_ref, wqt_ref, k1_ref, k2_ref, xnt_ref, r2_ref, bz_ref, cnt_ref, a_ref,
                   qt_scr, c_scr):
    d = x_ref.shape[1]
    x = x_ref[...]
    inv = lax.rsqrt(jnp.mean(x * x, axis=-1, keepdims=True) + NORM_EPS)
    for c0 in range(0, d, XPOSE_COLS):
        cols = slice(c0, c0 + XPOSE_COLS)
        xn_c = x_ref[:, cols] * inv * g_ref[:, cols]
        xnt_ref[cols, :] = xn_c.T.astype(BF16)
    qt_scr[...] = jnp.dot(wqt_ref[...], xnt_ref[...],
                          preferred_element_type=F32).astype(BF16)
    c_scr[...] = jnp.full(c_scr.shape, NEG_INF, F32)

    def head(h, carry):
        off = pl.multiple_of(h * PEER_QDIM, PEER_QDIM)
        s1 = jnp.dot(k1_ref[...], qt_scr[pl.ds(off, PEER_HALF), :], preferred_element_type=F32)
        s2 = jnp.dot(k2_ref[...], qt_scr[pl.ds(off + PEER_HALF, PEER_HALF), :],
                     preferred_element_type=F32)
        v1 = _top_values(s1, N_TOP)
        v2 = _top_values(s2, N_TOP)
        for r, (a, b) in enumerate(CAND):
            c_scr[r:r + 1, :] = v1[a] + v2[b]
        c0 = c_scr[...]
        csort = _top_values(c0, N_TOP, one_at_a_time=True)
        thr = 0.5 * (csort[PEER_TOPK - 1] + csort[PEER_TOPK])
        top = v1[0] + v2[0]
        z = jnp.sum(jnp.where(c0 >= thr, jnp.exp(c0 - top), 0.0), axis=0, keepdims=True)
        r2_ref[h] = _count_above(v2[:PEER_TOPK], s2, strict=True).astype(BF16)
        bz_ref[h] = (jnp.exp(s2 - v2[0]) / z).astype(BF16)
        cnt_ref[h] = _count_above(v2[:PEER_TOPK], thr - s1, strict=False)
        a_ref[h] = jnp.exp(s1 - v1[0])
        return carry

    lax.fori_loop(0, PEER_HEADS, head, 0)


def peer_router(x, g, wqt, k1, k2, *, bm):
    t, d = x.shape
    head_spec = pl.BlockSpec((PEER_HEADS, N_KEYS, bm), lambda i: (0, 0, i))
    return pl.pallas_call(
        _router_kernel,
        grid=(t // bm,),
        in_specs=[
            pl.BlockSpec((bm, d), lambda i: (i, 0)),
            pl.BlockSpec((1, d), lambda i: (0, 0)),
            pl.BlockSpec((PEER_HEADS * PEER_QDIM, d), lambda i: (0, 0)),
            pl.BlockSpec((N_KEYS, PEER_HALF), lambda i: (0, 0)),
            pl.BlockSpec((N_KEYS, PEER_HALF), lambda i: (0, 0)),
        ],
        out_specs=[pl.BlockSpec((d, bm), lambda i: (0, i)), head_spec, head_spec, head_spec, head_spec],
        out_shape=[jax.ShapeDtypeStruct((d, t), BF16),
                   jax.ShapeDtypeStruct((PEER_HEADS, N_KEYS, t), BF16),
                   jax.ShapeDtypeStruct((PEER_HEADS, N_KEYS, t), BF16),
                   jax.ShapeDtypeStruct((PEER_HEADS, N_KEYS, t), F32),
                   jax.ShapeDtypeStruct((PEER_HEADS, N_KEYS, t), F32)],
        scratch_shapes=[pltpu.VMEM((PEER_HEADS * PEER_QDIM, bm), BF16), pltpu.VMEM((CAND_ROWS, bm), F32)],
        compiler_params=_cparams("parallel"),
        name="peer_router",
    )(x, g.reshape(1, d), wqt, k1, k2)


EXPERT_TILE = 2048
EXPERT_VMEM_LIMIT = 62 * 1024 * 1024
EXPERT_CHUNK = 256
KEYS_PER_TILE = EXPERT_TILE // N_KEYS
KEYS_PER_CHUNK = EXPERT_CHUNK // N_KEYS
CHUNKS_PER_TILE = EXPERT_TILE // EXPERT_CHUNK
TOKEN_GROUP = 2 * LANES
assert KEYS_PER_TILE % SUBLANES == 0


def _expert_kernel(xnt_ref, u_ref, vt_ref, r2_ref, bz_ref, cnt_ref, a_ref, x_hbm, gf_ref, y_ref, p_scr,
                   acc_scr, x_sem, *, tile_off, final_norm):
    e = pl.program_id(1)
    bm = y_ref.shape[0]

    def x_copy():
        row0 = pl.multiple_of((pl.program_id(0) + tile_off) * bm, bm)
        return pltpu.make_async_copy(x_hbm.at[pl.ds(row0, bm), :], y_ref, x_sem)

    @pl.when(e == 0)
    def _():
        acc_scr[...] = jnp.zeros(acc_scr.shape, F32)
        x_copy().start()

    for c in range(CHUNKS_PER_TILE):
        crow = slice(c * EXPERT_CHUNK, (c + 1) * EXPERT_CHUNK)
        h = jnp.dot(u_ref[crow, :], xnt_ref[...], preferred_element_type=F32)
        for kk in range(KEYS_PER_CHUNK):
            il = c * KEYS_PER_CHUNK + kk
            rows = slice(il * N_KEYS, (il + 1) * N_KEYS)
            for tg in range(bm // TOKEN_GROUP):
                ts = slice(tg * TOKEN_GROUP, (tg + 1) * TOKEN_GROUP)
                g = jax.nn.gelu(h[kk * N_KEYS:(kk + 1) * N_KEYS, ts].astype(BF16))
                p = jnp.zeros((N_KEYS, TOKEN_GROUP), BF16)
                for hd in range(PEER_HEADS):
                    cb = jnp.broadcast_to(cnt_ref[hd, il:il + 1, ts], (N_KEYS, TOKEN_GROUP)).astype(BF16)
                    ab = jnp.broadcast_to(a_ref[hd, il:il + 1, ts], (N_KEYS, TOKEN_GROUP)).astype(BF16)
                    p = p + jnp.where(r2_ref[hd, :, ts] < cb, bz_ref[hd, :, ts], 0.0) * ab
                p_scr[rows, ts] = p * g
    acc_scr[...] += jnp.dot(vt_ref[...], p_scr[...], preferred_element_type=F32)

    @pl.when(e == pl.num_programs(1) - 1)
    def _():
        x_copy().wait()
        for c0 in range(0, y_ref.shape[1], XPOSE_COLS):
            cols = slice(c0, c0 + XPOSE_COLS)
            y_ref[:, cols] += acc_scr[cols, :].T
        if final_norm:
            y_ref[...] = _rms(y_ref[...], gf_ref[...])


def peer_experts(xnt, u, vt, r2, bz, cnt, a, x, g_final, *, layer, bm, start=0, rows=None, final_norm=False):
    d = x.shape[1]
    rows = x.shape[0] if rows is None else rows
    assert bm % TOKEN_GROUP == 0 and start % bm == 0 and rows % bm == 0
    off = start // bm
    head_full = pl.BlockSpec((PEER_HEADS, N_KEYS, bm), lambda i, e: (0, 0, i + off))
    head_rows = pl.BlockSpec((PEER_HEADS, KEYS_PER_TILE, bm), lambda i, e: (0, e, i + off))
    return pl.pallas_call(
        functools.partial(_expert_kernel, tile_off=off, final_norm=final_norm),
        grid=(rows // bm, u.shape[1] // EXPERT_TILE),
        in_specs=[
            pl.BlockSpec((d, bm), lambda i, e: (0, i + off)),
            pl.BlockSpec((None, EXPERT_TILE, d), lambda i, e: (layer, e, 0)),
            pl.BlockSpec((None, d, EXPERT_TILE), lambda i, e: (layer, 0, e)),
            head_full, head_full, head_rows, head_rows,
            pl.BlockSpec(memory_space=pl.ANY),
            pl.BlockSpec((1, d), lambda i, e: (0, 0)),
        ],
        out_specs=pl.BlockSpec((bm, d), lambda i, e: (i, 0)),
        out_shape=jax.ShapeDtypeStruct((rows, d), F32),
        scratch_shapes=[
            pltpu.VMEM((EXPERT_TILE, bm), BF16),
            pltpu.VMEM((d, bm), F32),
            pltpu.SemaphoreType.DMA(()),
        ],
        compiler_params=pltpu.CompilerParams(dimension_semantics=("parallel", "arbitrary"),
                                             vmem_limit_bytes=EXPERT_VMEM_LIMIT),
        name="peer_experts",
    )(xnt, u, vt, r2, bz, cnt, a, x, g_final.reshape(1, d))


def _block_meta(groups, blk):
    pos, first, last, batch = [], [], [], []
    b0 = 0
    for (b, s) in groups:
        per = s // blk
        for bi in range(b):
            for j in range(per):
                pos.append(j)
                first.append(int(j == 0))
                last.append(int(j == per - 1))
                batch.append(b0 + bi)
        b0 += b
    return jnp.asarray(np.array([pos, first, last, batch], dtype=np.int32))


def _rope_table(max_pos):
    half = HEAD_DIM // 2
    inv = ROPE_THETA ** (-jnp.arange(half, dtype=F32) / half)
    ang = jnp.arange(max_pos, dtype=F32)[:, None] * inv[None, :]
    cos, sin = jnp.cos(ang), jnp.sin(ang)
    return jnp.concatenate([cos, cos, -sin, sin], axis=-1)


def _pick(t, pref):
    while t % pref:
        pref //= 2
    return pref


def _trunk(xs, mems, mix_norm, w_in, attn_sink, gm_ln_g, gm_ln_b, gm_w_s, gm_b_s, w_att_proj, w_gm_proj,
           w_out, cross_norm, mem_norm, w_q_mem, w_kv_mem, w_o_mem, peer_norm, w_q_peer, peer_k1, peer_k2,
           expert_u, expert_v, final_norm):
    groups = [(x.shape[0], x.shape[1]) for x in xs]
    depth = w_in.shape[0]
    for (_, s) in groups:
        assert s % ATT_BLOCK == 0
    x_parts = tuple(x.reshape(-1, D_MODEL) for x in xs)
    if len(x_parts) > 2:
        x_parts = (jnp.concatenate(x_parts, axis=0),)
    mem = jnp.concatenate([m.reshape(-1, D_MODEL) for m in mems], axis=0)
    n_mem_rows = mem.shape[0]
    meta = _block_meta(groups, ATT_BLOCK)
    cs_table = _rope_table(max(s for _, s in groups))

    common = int(np.gcd.reduce([b * s for b, s in groups]))
    bm_big = _pick(common, 1024)
    bm_mid = _pick(common, 512)
    bm_small = _pick(common, 256)
    u_all = expert_u.astype(BF16)
    vt_all = jnp.swapaxes(expert_v, 1, 2).astype(BF16)
    in_splits = np.cumsum([ATT_WIDTH, KV_WIDTH, KV_WIDTH, GM_WIDTH, GM_WIDTH, D_MODEL]).tolist()

    for l in range(depth):
        q_w, k_w, v_w, gu_w, gz_w, ga_w, gb_w = jnp.split(w_in[l], in_splits, axis=1)
        w_in_p = jnp.concatenate([q_w, gu_w, gz_w, ga_w, gb_w, k_w, v_w], axis=1).astype(BF16)
        sink_col = jnp.broadcast_to((attn_sink[l].astype(F32) * LOG2_E)[:, None, None],
                                    (N_Q_HEADS, WINDOW, SINK_LANES))
        b_s_b = jnp.broadcast_to(gm_b_s[l].astype(F32)[:, :, None],
                                 (GM_GROUPS, GM_CHUNK, GM_WIDTH // GM_GROUPS))

        proj = norm_matmul(x_parts, mix_norm[l], w_in_p, bm=bm_big, bn=1024)
        att = banded_attention(proj, meta, cs_table, sink_col)
        gm = spatial_gating(proj, gm_ln_g[l], gm_ln_b[l], gm_w_s[l].astype(BF16), b_s_b, bm=ATT_BLOCK)
        x = merge_project(att, gm, proj, x_parts, w_att_proj[l].astype(BF16), w_gm_proj[l].astype(BF16),
                          w_out[l].astype(BF16), bm=bm_small)

        kv = norm_matmul((mem,), mem_norm[l], w_kv_mem[l].astype(BF16), bm=_pick(n_mem_rows, 512), bn=1024)
        kv = kv.reshape(-1, MEM_TOKENS, 2 * MEM_WIDTH)
        x = cross_attention(x, meta, cross_norm[l], w_q_mem[l].astype(BF16), kv, w_o_mem[l].astype(BF16),
                            bm=ATT_BLOCK)

        xnt, r2, bz, cnt, a = peer_router(x, peer_norm[l], w_q_peer[l].T.astype(BF16),
                                          peer_k1[l].astype(BF16), peer_k2[l].astype(BF16), bm=bm_mid)
        experts = functools.partial(peer_experts, xnt, u_all, vt_all, r2, bz, cnt, a, x, final_norm,
                                    layer=l, bm=bm_mid)
        if l + 1 < depth:
            x_parts = (experts(),)
    outs = []
    start = 0
    for (b, s) in groups:
        y = experts(start=start, rows=b * s, final_norm=True)
        outs.append(y.reshape(b, s, D_MODEL))
        start += b * s
    return tuple(outs)


def kernel(x_prompt, x_sample, mem_prompt, mem_sample, mix_norm, w_in, attn_sink, gm_ln_g, gm_ln_b, gm_w_s, gm_b_s, w_att_proj, w_gm_proj, w_out, cross_norm, mem_norm, w_q_mem, w_kv_mem, w_o_mem, peer_norm, w_q_peer, peer_k1, peer_k2, expert_u, expert_v, final_norm):
    return _trunk((x_prompt, x_sample), (mem_prompt, mem_sample), mix_norm, w_in, attn_sink, gm_ln_g,
                  gm_ln_b, gm_w_s, gm_b_s, w_att_proj, w_gm_proj, w_out, cross_norm, mem_norm, w_q_mem,
                  w_kv_mem, w_o_mem, peer_norm, w_q_peer, peer_k1, peer_k2, expert_u, expert_v, final_norm)
```

```python
import functools

import numpy as np
import jax
import jax.numpy as jnp
from jax import lax
from jax.experimental import pallas as pl
from jax.experimental.pallas import tpu as pltpu

F32 = jnp.float32
BF16 = jnp.bfloat16

D_MODEL = 2048
HEAD_DIM = 128
N_Q_HEADS = 16
N_KV_HEADS = 4
Q_PER_KV = N_Q_HEADS // N_KV_HEADS
ATT_WIDTH = N_Q_HEADS * HEAD_DIM
KV_WIDTH = N_KV_HEADS * HEAD_DIM
WINDOW = 128
ROPE_THETA = 10000.0
GM_WIDTH = 2048
GM_GROUPS = 16
GM_CHUNK = 128
MEM_TOKENS = 256
MEM_HEADS = 4
MEM_HEAD_DIM = 128
MEM_WIDTH = MEM_HEADS * MEM_HEAD_DIM
PEER_HEADS = 8
PEER_QDIM = 256
PEER_HALF = PEER_QDIM // 2
N_KEYS = 128
N_EXPERTS = N_KEYS * N_KEYS
PEER_TOPK = 16
NORM_EPS = 1e-6
NEG_INF = -1e30
LOG2_E = 1.4426950408889634

COL_Q, COL_GU, COL_GZ, COL_GA, COL_GB = 0, 1, 2, 3, 4
COL_K = (ATT_WIDTH + 4 * D_MODEL) // KV_WIDTH
COL_V = COL_K + 1

LANES = 128
SUBLANES = 8
ATT_BLOCK = 512
SINK_LANES = LANES
HEADS_PER_PASS = 2
VMEM_LIMIT = 56 * 1024 * 1024


def _cparams(*sem):
    return pltpu.CompilerParams(dimension_semantics=sem, vmem_limit_bytes=VMEM_LIMIT)


def _rms(x, g):
    return x * lax.rsqrt(jnp.mean(x * x, axis=-1, keepdims=True) + NORM_EPS) * g


def _dot_nt(a, b):
    return lax.dot_general(a, b, (((1,), (1,)), ((), ())), preferred_element_type=F32)


def _row_parts(xs, bm):
    xa, xb = (xs[0], xs[0]) if len(xs) == 1 else xs
    assert all(x.shape[0] % bm == 0 for x in xs)
    return xa, xb, xa.shape[0] // bm, sum(x.shape[0] for x in xs)


def _part_specs(bm, d, na, ndim_grid):
    if ndim_grid == 1:
        return [pl.BlockSpec((bm, d), lambda i: (jnp.minimum(i, na - 1), 0)),
                pl.BlockSpec((bm, d), lambda i: (jnp.maximum(i - na, 0), 0))]
    return [pl.BlockSpec((bm, d), lambda i, j: (jnp.minimum(i, na - 1), 0)),
            pl.BlockSpec((bm, d), lambda i, j: (jnp.maximum(i - na, 0), 0))]


def _norm_matmul_kernel(xa_ref, xb_ref, g_ref, w_ref, o_ref, xn_ref, *, na):
    i = pl.program_id(0)
    first = pl.program_id(1) == 0

    @pl.when(jnp.logical_and(first, i < na))
    def _():
        xn_ref[...] = _rms(xa_ref[...], g_ref[...]).astype(BF16)

    @pl.when(jnp.logical_and(first, i >= na))
    def _():
        xn_ref[...] = _rms(xb_ref[...], g_ref[...]).astype(BF16)

    o_ref[...] = jnp.dot(xn_ref[...], w_ref[...], preferred_element_type=F32).astype(o_ref.dtype)


def norm_matmul(xs, g, w, *, bm, bn):
    xa, xb, na, t = _row_parts(xs, bm)
    d = xa.shape[1]
    n = w.shape[1]
    assert n % bn == 0
    return pl.pallas_call(
        functools.partial(_norm_matmul_kernel, na=na),
        grid=(t // bm, n // bn),
        in_specs=_part_specs(bm, d, na, 2) + [
            pl.BlockSpec((1, d), lambda i, j: (0, 0)),
            pl.BlockSpec((d, bn), lambda i, j: (0, j)),
        ],
        out_specs=pl.BlockSpec((bm, bn), lambda i, j: (i, j)),
        out_shape=jax.ShapeDtypeStruct((t, n), BF16),
        scratch_shapes=[pltpu.VMEM((bm, d), BF16)],
        compiler_params=_cparams("parallel", "arbitrary"),
        name="norm_matmul",
    )(xa, xb, g.reshape(1, d), w)


def _attn_kernel(meta_ref, q_ref, kc_ref, kp_ref, kn_ref, vc_ref, vp_ref, vn_ref,
                 csc_ref, csp_ref, csn_ref, sink_ref, o_ref, q_scr, k_scr, v_scr, o_scr):
    n = pl.program_id(0)
    nb = ATT_BLOCK // WINDOW
    band = 3 * WINDOW
    k_lo = jnp.where(meta_ref[1, n] == 1, WINDOW, 0)
    k_hi = jnp.where(meta_ref[2, n] == 1, 2 * WINDOW, band)

    def rope(x, cs):
        return x * cs[:, :HEAD_DIM] + pltpu.roll(x, HEAD_DIM // 2, 1) * cs[:, HEAD_DIM:]

    cs_c = csc_ref[...]
    for hq in range(N_Q_HEADS):
        sl = slice(hq * HEAD_DIM, (hq + 1) * HEAD_DIM)
        q_scr[hq] = rope(q_ref[:, sl].astype(F32), cs_c).astype(BF16)
    cs_p = csp_ref[...]
    cs_n = csn_ref[...]
    for h in range(N_KV_HEADS):
        sl = slice(h * HEAD_DIM, (h + 1) * HEAD_DIM)
        k_scr[h, 0:WINDOW, :] = rope(kp_ref[:, sl].astype(F32), cs_p).astype(BF16)
        k_scr[h, WINDOW:WINDOW + ATT_BLOCK, :] = rope(kc_ref[:, sl].astype(F32), cs_c).astype(BF16)
        k_scr[h, WINDOW + ATT_BLOCK:, :] = rope(kn_ref[:, sl].astype(F32), cs_n).astype(BF16)
        v_scr[h, 0:WINDOW, :] = vp_ref[:, sl]
        v_scr[h, WINDOW:WINDOW + ATT_BLOCK, :] = vc_ref[:, sl]
        v_scr[h, WINDOW + ATT_BLOCK:, :] = vn_ref[:, sl]

    qi = lax.broadcasted_iota(jnp.int32, (WINDOW, band), 0)
    kc = lax.broadcasted_iota(jnp.int32, (WINDOW, band), 1)
    rel = jnp.abs(qi + WINDOW - kc) <= WINDOW
    scale = HEAD_DIM ** -0.5 * LOG2_E

    def band_step(r, carry):
        r0 = pl.multiple_of(r * WINDOW, WINDOW)
        lo = jnp.where(r == 0, k_lo, 0)
        hi = jnp.where(r == nb - 1, k_hi, band)
        bias = jnp.where(jnp.logical_and(rel, jnp.logical_and(kc >= lo, kc < hi)), 0.0, NEG_INF)
        for h0 in range(0, N_KV_HEADS, HEADS_PER_PASS):
            heads = range(h0, h0 + HEADS_PER_PASS)
            kbs = {h: k_scr[h, pl.ds(r0, band), :] for h in heads}
            vbs = {h: v_scr[h, pl.ds(r0, band), :] for h in heads}
            scores = {(h, g): _dot_nt(q_scr[Q_PER_KV * h + g, pl.ds(r0, WINDOW), :], kbs[h])
                      for h in heads for g in range(Q_PER_KV)}
            for h in heads:
                for g in range(Q_PER_KV):
                    hq = Q_PER_KV * h + g
                    s = scores[(h, g)] * scale + bias
                    sk = sink_ref[hq]
                    m = jnp.maximum(jnp.max(s, axis=-1, keepdims=True), sk)
                    e = jnp.concatenate([jnp.exp2(s[:, j * WINDOW:(j + 1) * WINDOW] - m) for j in range(3)],
                                        axis=-1)
                    den = jnp.sum(e, axis=-1, keepdims=True) + jnp.exp2(sk - m)
                    o = jnp.dot(e.astype(BF16), vbs[h], preferred_element_type=F32) * (1.0 / den)
                    o_scr[hq, pl.ds(r0, WINDOW), :] = o.astype(BF16)
        return carry

    lax.fori_loop(0, nb, band_step, 0)
    for hq in range(N_Q_HEADS):
        o_ref[:, hq * HEAD_DIM:(hq + 1) * HEAD_DIM] = o_scr[hq]


def banded_attention(proj, meta, cs_table, sink_col):
    t = proj.shape[0]
    nblk = t // ATT_BLOCK
    sub = ATT_BLOCK // WINDOW
    n_small = t // WINDOW
    n_pos_small = cs_table.shape[0] // WINDOW
    grid_spec = pltpu.PrefetchScalarGridSpec(
        num_scalar_prefetch=1,
        grid=(nblk,),
        in_specs=[
            pl.BlockSpec((ATT_BLOCK, ATT_WIDTH), lambda n, m: (n, COL_Q)),
            pl.BlockSpec((ATT_BLOCK, KV_WIDTH), lambda n, m: (n, COL_K)),
            pl.BlockSpec((WINDOW, KV_WIDTH), lambda n, m: (jnp.maximum(n * sub - 1, 0), COL_K)),
            pl.BlockSpec((WINDOW, KV_WIDTH), lambda n, m: (jnp.minimum(n * sub + sub, n_small - 1), COL_K)),
            pl.BlockSpec((ATT_BLOCK, KV_WIDTH), lambda n, m: (n, COL_V)),
            pl.BlockSpec((WINDOW, KV_WIDTH), lambda n, m: (jnp.maximum(n * sub - 1, 0), COL_V)),
            pl.BlockSpec((WINDOW, KV_WIDTH), lambda n, m: (jnp.minimum(n * sub + sub, n_small - 1), COL_V)),
            pl.BlockSpec((ATT_BLOCK, 2 * HEAD_DIM), lambda n, m: (m[0, n], 0)),
            pl.BlockSpec((WINDOW, 2 * HEAD_DIM), lambda n, m: (jnp.maximum(m[0, n] * sub - 1, 0), 0)),
            pl.BlockSpec((WINDOW, 2 * HEAD_DIM),
                         lambda n, m: (jnp.minimum(m[0, n] * sub + sub, n_pos_small - 1), 0)),
            pl.BlockSpec((N_Q_HEADS, WINDOW, SINK_LANES), lambda n, m: (0, 0, 0)),
        ],
        out_specs=pl.BlockSpec((ATT_BLOCK, ATT_WIDTH), lambda n, m: (n, 0)),
        scratch_shapes=[
            pltpu.VMEM((N_Q_HEADS, ATT_BLOCK, HEAD_DIM), BF16),
            pltpu.VMEM((N_KV_HEADS, ATT_BLOCK + 2 * WINDOW, HEAD_DIM), BF16),
            pltpu.VMEM((N_KV_HEADS, ATT_BLOCK + 2 * WINDOW, HEAD_DIM), BF16),
            pltpu.VMEM((N_Q_HEADS, ATT_BLOCK, HEAD_DIM), BF16),
        ],
    )
    return pl.pallas_call(
        _attn_kernel,
        grid_spec=grid_spec,
        out_shape=jax.ShapeDtypeStruct((t, ATT_WIDTH), BF16),
        compiler_params=_cparams("parallel"),
        name="banded_attention",
    )(meta, proj, proj, proj, proj, proj, proj, proj, cs_table, cs_table, cs_table, sink_col)


def _gmlp_kernel(gu_ref, gz_ref, lng_ref, lnb_ref, ws_ref, bs_ref, o_ref, zn_scr, u_scr):
    z = jax.nn.gelu(gz_ref[...]).astype(F32)
    mu = jnp.mean(z, axis=-1, keepdims=True)
    zc = z - mu
    var = jnp.mean(zc * zc, axis=-1, keepdims=True)
    zn_scr[...] = (zc * lax.rsqrt(var + NORM_EPS) * lng_ref[...] + lnb_ref[...]).astype(BF16)
    u_scr[...] = jax.nn.gelu(gu_ref[...])
    gd = GM_WIDTH // GM_GROUPS
    for c in range(o_ref.shape[0] // GM_CHUNK):
        rs = slice(c * GM_CHUNK, (c + 1) * GM_CHUNK)
        for g in range(GM_GROUPS):
            cs = slice(g * gd, (g + 1) * gd)
            mixed = jnp.dot(ws_ref[g], zn_scr[rs, cs], preferred_element_type=F32) + bs_ref[g]
            o_ref[rs, cs] = (u_scr[rs, cs].astype(F32) * mixed).astype(BF16)


def spatial_gating(proj, ln_g, ln_b, w_s, b_s_b, *, bm):
    t = proj.shape[0]
    return pl.pallas_call(
        _gmlp_kernel,
        grid=(t // bm,),
        in_specs=[
            pl.BlockSpec((bm, GM_WIDTH), lambda i: (i, COL_GU)),
            pl.BlockSpec((bm, GM_WIDTH), lambda i: (i, COL_GZ)),
            pl.BlockSpec((1, GM_WIDTH), lambda i: (0, 0)),
            pl.BlockSpec((1, GM_WIDTH), lambda i: (0, 0)),
            pl.BlockSpec((GM_GROUPS, GM_CHUNK, GM_CHUNK), lambda i: (0, 0, 0)),
            pl.BlockSpec((GM_GROUPS, GM_CHUNK, GM_WIDTH // GM_GROUPS), lambda i: (0, 0, 0)),
        ],
        out_specs=pl.BlockSpec((bm, GM_WIDTH), lambda i: (i, 0)),
        out_shape=jax.ShapeDtypeStruct((t, GM_WIDTH), BF16),
        scratch_shapes=[pltpu.VMEM((bm, GM_WIDTH), BF16), pltpu.VMEM((bm, GM_WIDTH), BF16)],
        compiler_params=_cparams("parallel"),
        name="spatial_gating",
    )(proj, proj, ln_g.reshape(1, -1), ln_b.reshape(1, -1), w_s, b_s_b)


def _sigmoid(x):
    return 1.0 / (1.0 + jnp.exp(-x))


def _merge_kernel(att_ref, gm_ref, ga_ref, gb_ref, xa_ref, xb_ref, wa_ref, wg_ref, wo_ref, y_ref, *, na):
    i = pl.program_id(0)

    @pl.when(i < na)
    def _():
        y_ref[...] = xa_ref[...]

    @pl.when(i >= na)
    def _():
        y_ref[...] = xb_ref[...]

    a = jnp.dot(att_ref[...], wa_ref[...], preferred_element_type=F32)
    b = jnp.dot(gm_ref[...], wg_ref[...], preferred_element_type=F32)
    merged = _sigmoid(ga_ref[...].astype(F32)) * a + _sigmoid(gb_ref[...].astype(F32)) * b
    y_ref[...] += jnp.dot(merged.astype(BF16), wo_ref[...], preferred_element_type=F32)


def merge_project(att, gm, proj, xs, wa, wg, wo, *, bm):
    xa, xb, na, t = _row_parts(xs, bm)
    d = xa.shape[1]
    const = dict(pipeline_mode=pl.Buffered(1))
    return pl.pallas_call(
        functools.partial(_merge_kernel, na=na),
        grid=(t // bm,),
        in_specs=[
            pl.BlockSpec((bm, ATT_WIDTH), lambda i: (i, 0)),
            pl.BlockSpec((bm, GM_WIDTH), lambda i: (i, 0)),
            pl.BlockSpec((bm, d), lambda i: (i, COL_GA)),
            pl.BlockSpec((bm, d), lambda i: (i, COL_GB)),
        ] + _part_specs(bm, d, na, 1) + [
            pl.BlockSpec((ATT_WIDTH, d), lambda i: (0, 0), **const),
            pl.BlockSpec((GM_WIDTH, d), lambda i: (0, 0), **const),
            pl.BlockSpec((d, d), lambda i: (0, 0), **const),
        ],
        out_specs=pl.BlockSpec((bm, d), lambda i: (i, 0)),
        out_shape=jax.ShapeDtypeStruct((t, d), F32),
        compiler_params=_cparams("parallel"),
        name="merge_project",
    )(att, gm, proj, proj, xa, xb, wa, wg, wo)


def _cross_kernel(meta_ref, x_ref, g_ref, wq_ref, kv_ref, wo_ref, y_ref):
    x = x_ref[...]
    xn = _rms(x, g_ref[...]).astype(BF16)
    q = jnp.dot(xn, wq_ref[...], preferred_element_type=F32).astype(BF16)
    scale = MEM_HEAD_DIM ** -0.5
    outs = []
    for h in range(MEM_HEADS):
        sl = slice(h * MEM_HEAD_DIM, (h + 1) * MEM_HEAD_DIM)
        kh = kv_ref[0, :, sl]
        vh = kv_ref[0, :, MEM_WIDTH + h * MEM_HEAD_DIM:MEM_WIDTH + (h + 1) * MEM_HEAD_DIM]
        s = _dot_nt(q[:, sl], kh) * scale
        m = jnp.max(s, axis=-1, keepdims=True)
        e = jnp.exp(s - m)
        p = (e / jnp.sum(e, axis=-1, keepdims=True)).astype(BF16)
        outs.append(jnp.dot(p, vh, preferred_element_type=F32).astype(BF16))
    o = jnp.concatenate(outs, axis=-1)
    y_ref[...] = x + jnp.dot(o, wo_ref[...], preferred_element_type=F32)


def cross_attention(x, meta, g, wq, kv, wo, *, bm):
    t, d = x.shape
    grid_spec = pltpu.PrefetchScalarGridSpec(
        num_scalar_prefetch=1,
        grid=(t // bm,),
        in_specs=[
            pl.BlockSpec((bm, d), lambda i, m: (i, 0)),
            pl.BlockSpec((1, d), lambda i, m: (0, 0)),
            pl.BlockSpec((d, MEM_WIDTH), lambda i, m: (0, 0)),
            pl.BlockSpec((1, MEM_TOKENS, 2 * MEM_WIDTH), lambda i, m: (m[3, i], 0, 0)),
            pl.BlockSpec((MEM_WIDTH, d), lambda i, m: (0, 0)),
        ],
        out_specs=pl.BlockSpec((bm, d), lambda i, m: (i, 0)),
    )
    return pl.pallas_call(
        _cross_kernel,
        grid_spec=grid_spec,
        out_shape=jax.ShapeDtypeStruct((t, d), F32),
        compiler_params=_cparams("parallel"),
        name="cross_attention",
    )(meta, x, g.reshape(1, d), wq, kv, wo)


N_TOP = PEER_TOPK + 1
CAND = [(a, b) for a in range(N_TOP) for b in range(N_TOP // (a + 1))]
CAND_ROWS = 8 * SUBLANES
assert len(CAND) <= CAND_ROWS
XPOSE_COLS = 256


def _oddeven_merge_sort(n):
    def merge(lo, hi, r):
        step = r * 2
        if step < hi - lo:
            yield from merge(lo, hi, step)
            yield from merge(lo + r, hi, step)
            yield from [(i, i + r) for i in range(lo + r, hi - r, step)]
        else:
            yield (lo, lo + r)

    def sort(lo, hi):
        if hi - lo >= 1:
            mid = lo + (hi - lo) // 2
            yield from sort(lo, mid)
            yield from sort(mid + 1, hi)
            yield from merge(lo, hi, 1)

    return list(sort(0, n - 1))


def _top_values(s, n_top, one_at_a_time=False):
    nblk = s.shape[0] // SUBLANES
    v = [s[d * SUBLANES:(d + 1) * SUBLANES, :] for d in range(nblk)]
    for a, b in _oddeven_merge_sort(nblk):
        v[a], v[b] = jnp.maximum(v[a], v[b]), jnp.minimum(v[a], v[b])
    sub = lax.broadcasted_iota(jnp.int32, v[0].shape, 0)
    vals = []
    for k in range(n_top):
        m = jnp.max(v[0], axis=0, keepdims=True)
        vals.append(m)
        remaining = n_top - 1 - k
        if remaining == 0:
            break
        hit = v[0] == m
        if one_at_a_time:
            first = jnp.min(jnp.where(hit, sub, SUBLANES), axis=0, keepdims=True)
            hit = sub == first
        for d in range(min(nblk, remaining)):
            v[d] = jnp.where(hit, v[d + 1] if d + 1 < nblk else NEG_INF, v[d])
    return vals


def _count_above(vals, x, strict):
    assert len(vals) == 16

    def above(v):
        return (v > x) if strict else (v >= x)

    bits = []

    def pivot(level, lo, size, depth=0):
        mid = lo + size // 2
        if depth == level:
            return vals[mid]
        return jnp.where(bits[depth], pivot(level, mid + 1, size // 2, depth + 1),
                         pivot(level, lo, size // 2, depth + 1))

    count = jnp.where(above(vals[15]), 1.0, 0.0)
    for level in range(4):
        bits.append(above(pivot(level, 0, 15)))
        count = count + jnp.where(bits[level], float(8 >> level), 0.0)
    return count


def _router_kernel(x_ref, g_ref, wqt_ref, k1_ref, k2_ref, xnt_ref, r2_ref, bz_ref, cnt_ref, a_ref,
                   qt_scr, c_scr):
    d = x_ref.shape[1]
    x = x_ref[...]
    inv = lax.rsqrt(jnp.mean(x * x, axis=-1, keepdims=True) + NORM_EPS)
    for c0 in range(0, d, XPOSE_COLS):
        cols = slice(c0, c0 + XPOSE_COLS)
        xn_c = x_ref[:, cols] * inv * g_ref[:, cols]
        xnt_ref[cols, :] = xn_c.T.astype(BF16)
    qt_scr[...] = jnp.dot(wqt_ref[...], xnt_ref[...],
                          preferred_element_type=F32).astype(BF16)
    c_scr[...] = jnp.full(c_scr.shape, NEG_INF, F32)

    def head(h, carry):
        off = pl.multiple_of(h * PEER_QDIM, PEER_QDIM)
        s1 = jnp.dot(k1_ref[...], qt_scr[pl.ds(off, PEER_HALF), :], preferred_element_type=F32)
        s2 = jnp.dot(k2_ref[...], qt_scr[pl.ds(off + PEER_HALF, PEER_HALF), :],
                     preferred_element_type=F32)
        v1 = _top_values(s1, N_TOP)
        v2 = _top_values(s2, N_TOP)
        for r, (a, b) in enumerate(CAND):
            c_scr[r:r + 1, :] = v1[a] + v2[b]
        c0 = c_scr[...]
        csort = _top_values(c0, N_TOP, one_at_a_time=True)
        thr = 0.5 * (csort[PEER_TOPK - 1] + csort[PEER_TOPK])
        top = v1[0] + v2[0]
        z = jnp.sum(jnp.where(c0 >= thr, jnp.exp(c0 - top), 0.0), axis=0, keepdims=True)
        r2_ref[h] = _count_above(v2[:PEER_TOPK], s2, strict=True).astype(BF16)
        bz_ref[h] = (jnp.exp(s2 - v2[0]) / z).astype(BF16)
        cnt_ref[h] = _count_above(v2[:PEER_TOPK], thr - s1, strict=False)
        a_ref[h] = jnp.exp(s1 - v1[0])
        return carry

    lax.fori_loop(0, PEER_HEADS, head, 0)


def peer_router(x, g, wqt, k1, k2, *, bm):
    t, d = x.shape
    head_spec = pl.BlockSpec((PEER_HEADS, N_KEYS, bm), lambda i: (0, 0, i))
    return pl.pallas_call(
        _router_kernel,
        grid=(t // bm,),
        in_specs=[
            pl.BlockSpec((bm, d), lambda i: (i, 0)),
            pl.BlockSpec((1, d), lambda i: (0, 0)),
            pl.BlockSpec((PEER_HEADS * PEER_QDIM, d), lambda i: (0, 0)),
            pl.BlockSpec((N_KEYS, PEER_HALF), lambda i: (0, 0)),
            pl.BlockSpec((N_KEYS, PEER_HALF), lambda i: (0, 0)),
        ],
        out_specs=[pl.BlockSpec((d, bm), lambda i: (0, i)), head_spec, head_spec, head_spec, head_spec],
        out_shape=[jax.ShapeDtypeStruct((d, t), BF16),
                   jax.ShapeDtypeStruct((PEER_HEADS, N_KEYS, t), BF16),
                   jax.ShapeDtypeStruct((PEER_HEADS, N_KEYS, t), BF16),
                   jax.ShapeDtypeStruct((PEER_HEADS, N_KEYS, t), F32),
                   jax.ShapeDtypeStruct((PEER_HEADS, N_KEYS, t), F32)],
        scratch_shapes=[pltpu.VMEM((PEER_HEADS * PEER_QDIM, bm), BF16), pltpu.VMEM((CAND_ROWS, bm), F32)],
        compiler_params=_cparams("parallel"),
        name="peer_router",
    )(x, g.reshape(1, d), wqt, k1, k2)


EXPERT_TILE = 2048
EXPERT_VMEM_LIMIT = 62 * 1024 * 1024
EXPERT_CHUNK = 256
KEYS_PER_TILE = EXPERT_TILE // N_KEYS
KEYS_PER_CHUNK = EXPERT_CHUNK // N_KEYS
CHUNKS_PER_TILE = EXPERT_TILE // EXPERT_CHUNK
TOKEN_GROUP = 2 * LANES
assert KEYS_PER_TILE % SUBLANES == 0


def _expert_kernel(xnt_ref, u_ref, vt_ref, r2_ref, bz_ref, cnt_ref, a_ref, x_hbm, gf_ref, y_ref, p_scr,
                   acc_scr, x_sem, *, tile_off, final_norm):
    e = pl.program_id(1)
    bm = y_ref.shape[0]

    def x_copy():
        row0 = pl.multiple_of((pl.program_id(0) + tile_off) * bm, bm)
        return pltpu.make_async_copy(x_hbm.at[pl.ds(row0, bm), :], y_ref, x_sem)

    @pl.when(e == 0)
    def _():
        acc_scr[...] = jnp.zeros(acc_scr.shape, F32)
        x_copy().start()

    for c in range(CHUNKS_PER_TILE):
        crow = slice(c * EXPERT_CHUNK, (c + 1) * EXPERT_CHUNK)
        h = jnp.dot(u_ref[crow, :], xnt_ref[...], preferred_element_type=F32)
        for kk in range(KEYS_PER_CHUNK):
            il = c * KEYS_PER_CHUNK + kk
            rows = slice(il * N_KEYS, (il + 1) * N_KEYS)
            for tg in range(bm // TOKEN_GROUP):
                ts = slice(tg * TOKEN_GROUP, (tg + 1) * TOKEN_GROUP)
                g = jax.nn.gelu(h[kk * N_KEYS:(kk + 1) * N_KEYS, ts].astype(BF16))
                p = jnp.zeros((N_KEYS, TOKEN_GROUP), BF16)
                for hd in range(PEER_HEADS):
                    cb = jnp.broadcast_to(cnt_ref[hd, il:il + 1, ts], (N_KEYS, TOKEN_GROUP)).astype(BF16)
                    ab = jnp.broadcast_to(a_ref[hd, il:il + 1, ts], (N_KEYS, TOKEN_GROUP)).astype(BF16)
                    p = p + jnp.where(r2_ref[hd, :, ts] < cb, bz_ref[hd, :, ts], 0.0) * ab
                p_scr[rows, ts] = p * g
    acc_scr[...] += jnp.dot(vt_ref[...], p_scr[...], preferred_element_type=F32)

    @pl.when(e == pl.num_programs(1) - 1)
    def _():
        x_copy().wait()
        for c0 in range(0, y_ref.shape[1], XPOSE_COLS):
            cols = slice(c0, c0 + XPOSE_COLS)
            y_ref[:, cols] += acc_scr[cols, :].T
        if final_norm:
            y_ref[...] = _rms(y_ref[...], gf_ref[...])


def peer_experts(xnt, u, vt, r2, bz, cnt, a, x, g_final, *, layer, bm, start=0, rows=None, final_norm=False):
    d = x.shape[1]
    rows = x.shape[0] if rows is None else rows
    assert bm % TOKEN_GROUP == 0 and start % bm == 0 and rows % bm == 0
    off = start // bm
    head_full = pl.BlockSpec((PEER_HEADS, N_KEYS, bm), lambda i, e: (0, 0, i + off))
    head_rows = pl.BlockSpec((PEER_HEADS, KEYS_PER_TILE, bm), lambda i, e: (0, e, i + off))
    return pl.pallas_call(
        functools.partial(_expert_kernel, tile_off=off, final_norm=final_norm),
        grid=(rows // bm, u.shape[1] // EXPERT_TILE),
        in_specs=[
            pl.BlockSpec((d, bm), lambda i, e: (0, i + off)),
            pl.BlockSpec((None, EXPERT_TILE, d), lambda i, e: (layer, e, 0)),
            pl.BlockSpec((None, d, EXPERT_TILE), lambda i, e: (layer, 0, e)),
            head_full, head_full, head_rows, head_rows,
            pl.BlockSpec(memory_space=pl.ANY),
            pl.BlockSpec((1, d), lambda i, e: (0, 0)),
        ],
        out_specs=pl.BlockSpec((bm, d), lambda i, e: (i, 0)),
        out_shape=jax.ShapeDtypeStruct((rows, d), F32),
        scratch_shapes=[
            pltpu.VMEM((EXPERT_TILE, bm), BF16),
            pltpu.VMEM((d, bm), F32),
            pltpu.SemaphoreType.DMA(()),
        ],
        compiler_params=pltpu.CompilerParams(dimension_semantics=("parallel", "arbitrary"),
                                             vmem_limit_bytes=EXPERT_VMEM_LIMIT),
        name="peer_experts",
    )(xnt, u, vt, r2, bz, cnt, a, x, g_final.reshape(1, d))


def _block_meta(groups, blk):
    pos, first, last, batch = [], [], [], []
    b0 = 0
    for (b, s) in groups:
        per = s // blk
        for bi in range(b):
            for j in range(per):
                pos.append(j)
                first.append(int(j == 0))
                last.append(int(j == per - 1))
                batch.append(b0 + bi)
        b0 += b
    return jnp.asarray(np.array([pos, first, last, batch], dtype=np.int32))


def _rope_table(max_pos):
    half = HEAD_DIM // 2
    inv = ROPE_THETA ** (-jnp.arange(half, dtype=F32) / half)
    ang = jnp.arange(max_pos, dtype=F32)[:, None] * inv[None, :]
    cos, sin = jnp.cos(ang), jnp.sin(ang)
    return jnp.concatenate([cos, cos, -sin, sin], axis=-1)


def _pick(t, pref):
    while t % pref:
        pref //= 2
    return pref


def _trunk(xs, mems, mix_norm, w_in, attn_sink, gm_ln_g, gm_ln_b, gm_w_s, gm_b_s, w_att_proj, w_gm_proj,
           w_out, cross_norm, mem_norm, w_q_mem, w_kv_mem, w_o_mem, peer_norm, w_q_peer, peer_k1, peer_k2,
           expert_u, expert_v, final_norm):
    groups = [(x.shape[0], x.shape[1]) for x in xs]
    depth = w_in.shape[0]
    for (_, s) in groups:
        assert s % ATT_BLOCK == 0
    x_parts = tuple(x.reshape(-1, D_MODEL) for x in xs)
    if len(x_parts) > 2:
        x_parts = (jnp.concatenate(x_parts, axis=0),)
    mem = jnp.concatenate([m.reshape(-1, D_MODEL) for m in mems], axis=0)
    n_mem_rows = mem.shape[0]
    meta = _block_meta(groups, ATT_BLOCK)
    cs_table = _rope_table(max(s for _, s in groups))

    common = int(np.gcd.reduce([b * s for b, s in groups]))
    bm_big = _pick(common, 1024)
    bm_mid = _pick(common, 512)
    bm_small = _pick(common, 256)
    u_all = expert_u.astype(BF16)
    vt_all = jnp.swapaxes(expert_v, 1, 2).astype(BF16)
    in_splits = np.cumsum([ATT_WIDTH, KV_WIDTH, KV_WIDTH, GM_WIDTH, GM_WIDTH, D_MODEL]).tolist()

    for l in range(depth):
        q_w, k_w, v_w, gu_w, gz_w, ga_w, gb_w = jnp.split(w_in[l], in_splits, axis=1)
        w_in_p = jnp.concatenate([q_w, gu_w, gz_w, ga_w, gb_w, k_w, v_w], axis=1).astype(BF16)
        sink_col = jnp.broadcast_to((attn_sink[l].astype(F32) * LOG2_E)[:, None, None],
                                    (N_Q_HEADS, WINDOW, SINK_LANES))
        b_s_b = jnp.broadcast_to(gm_b_s[l].astype(F32)[:, :, None],
                                 (GM_GROUPS, GM_CHUNK, GM_WIDTH // GM_GROUPS))

        proj = norm_matmul(x_parts, mix_norm[l], w_in_p, bm=bm_big, bn=1024)
        att = banded_attention(proj, meta, cs_table, sink_col)
        gm = spatial_gating(proj, gm_ln_g[l], gm_ln_b[l], gm_w_s[l].astype(BF16), b_s_b, bm=ATT_BLOCK)
        x = merge_project(att, gm, proj, x_parts, w_att_proj[l].astype(BF16), w_gm_proj[l].astype(BF16),
                          w_out[l].astype(BF16), bm=bm_small)

        kv = norm_matmul((mem,), mem_norm[l], w_kv_mem[l].astype(BF16), bm=_pick(n_mem_rows, 512), bn=1024)
        kv = kv.reshape(-1, MEM_TOKENS, 2 * MEM_WIDTH)
        x = cross_attention(x, meta, cross_norm[l], w_q_mem[l].astype(BF16), kv, w_o_mem[l].astype(BF16),
                            bm=ATT_BLOCK)

        xnt, r2, bz, cnt, a = peer_router(x, peer_norm[l], w_q_peer[l].T.astype(BF16),
                                          peer_k1[l].astype(BF16), peer_k2[l].astype(BF16), bm=bm_mid)
        experts = functools.partial(peer_experts, xnt, u_all, vt_all, r2, bz, cnt, a, x, final_norm,
                                    layer=l, bm=bm_mid)
        if l + 1 < depth:
            x_parts = (experts(),)
    outs = []
    start = 0
    for (b, s) in groups:
        y = experts(start=start, rows=b * s, final_norm=True)
        outs.append(y.reshape(b, s, D_MODEL))
        start += b * s
    return tuple(outs)


def kernel(x_prompt, x_sample, mem_prompt, mem_sample, mix_norm, w_in, attn_sink, gm_ln_g, gm_ln_b, gm_w_s, gm_b_s, w_att_proj, w_gm_proj, w_out, cross_norm, mem_norm, w_q_mem, w_kv_mem, w_o_mem, peer_norm, w_q_peer, peer_k1, peer_k2, expert_u, expert_v, final_norm):
    return _trunk((x_prompt, x_sample), (mem_prompt, mem_sample), mix_norm, w_in, attn_sink, gm_ln_g,
                  gm_ln_b, gm_w_s, gm_b_s, w_att_proj, w_gm_proj, w_out, cross_norm, mem_norm, w_q_mem,
                  w_kv_mem, w_o_mem, peer_norm, w_q_peer, peer_k1, peer_k2, expert_u, expert_v, final_norm)
```

```python
import functools

import numpy as np
import jax
import jax.numpy as jnp
from jax import lax
from jax.experimental import pallas as pl
from jax.experimental.pallas import tpu as pltpu

F32 = jnp.float32
BF16 = jnp.bfloat16

D_MODEL = 2048
HEAD_DIM = 128
N_Q_HEADS = 16
N_KV_HEADS = 4
Q_PER_KV = N_Q_HEADS // N_KV_HEADS
ATT_WIDTH = N_Q_HEADS * HEAD_DIM
KV_WIDTH = N_KV_HEADS * HEAD_DIM
WINDOW = 128
ROPE_THETA = 10000.0
GM_WIDTH = 2048
GM_GROUPS = 16
GM_CHUNK = 128
MEM_TOKENS = 256
MEM_HEADS = 4
MEM_HEAD_DIM = 128
MEM_WIDTH = MEM_HEADS * MEM_HEAD_DIM
PEER_HEADS = 8
PEER_QDIM = 256
PEER_HALF = PEER_QDIM // 2
N_KEYS = 128
N_EXPERTS = N_KEYS * N_KEYS
PEER_TOPK = 16
NORM_EPS = 1e-6
NEG_INF = -1e30
LOG2_E = 1.4426950408889634

COL_Q, COL_GU, COL_GZ, COL_GA, COL_GB = 0, 1, 2, 3, 4
COL_K = (ATT_WIDTH + 4 * D_MODEL) // KV_WIDTH
COL_V = COL_K + 1

LANES = 128
SUBLANES = 8
ATT_BLOCK = 512
SINK_LANES = LANES
HEADS_PER_PASS = 2
VMEM_LIMIT = 56 * 1024 * 1024


def _cparams(*sem):
    return pltpu.CompilerParams(dimension_semantics=sem, vmem_limit_bytes=VMEM_LIMIT)


def _rms(x, g):
    return x * lax.rsqrt(jnp.mean(x * x, axis=-1, keepdims=True) + NORM_EPS) * g


def _dot_nt(a, b):
    return lax.dot_general(a, b, (((1,), (1,)), ((), ())), preferred_element_type=F32)


def _row_parts(xs, bm):
    xa, xb = (xs[0], xs[0]) if len(xs) == 1 else xs
    assert all(x.shape[0] % bm == 0 for x in xs)
    return xa, xb, xa.shape[0] // bm, sum(x.shape[0] for x in xs)


def _part_specs(bm, d, na, ndim_grid):
    if ndim_grid == 1:
        return [pl.BlockSpec((bm, d), lambda i: (jnp.minimum(i, na - 1), 0)),
                pl.BlockSpec((bm, d), lambda i: (jnp.maximum(i - na, 0), 0))]
    return [pl.BlockSpec((bm, d), lambda i, j: (jnp.minimum(i, na - 1), 0)),
            pl.BlockSpec((bm, d), lambda i, j: (jnp.maximum(i - na, 0), 0))]


def _norm_matmul_kernel(xa_ref, xb_ref, g_ref, w_ref, o_ref, xn_ref, *, na):
    i = pl.program_id(0)
    first = pl.program_id(1) == 0

    @pl.when(jnp.logical_and(first, i < na))
    def _():
        xn_ref[...] = _rms(xa_ref[...], g_ref[...]).astype(BF16)

    @pl.when(jnp.logical_and(first, i >= na))
    def _():
        xn_ref[...] = _rms(xb_ref[...], g_ref[...]).astype(BF16)

    o_ref[...] = jnp.dot(xn_ref[...], w_ref[...], preferred_element_type=F32).astype(o_ref.dtype)


def norm_matmul(xs, g, w, *, bm, bn):
    xa, xb, na, t = _row_parts(xs, bm)
    d = xa.shape[1]
    n = w.shape[1]
    assert n % bn == 0
    return pl.pallas_call(
        functools.partial(_norm_matmul_kernel, na=na),
        grid=(t // bm, n // bn),
        in_specs=_part_specs(bm, d, na, 2) + [
            pl.BlockSpec((1, d), lambda i, j: (0, 0)),
            pl.BlockSpec((d, bn), lambda i, j: (0, j)),
        ],
        out_specs=pl.BlockSpec((bm, bn), lambda i, j: (i, j)),
        out_shape=jax.ShapeDtypeStruct((t, n), BF16),
        scratch_shapes=[pltpu.VMEM((bm, d), BF16)],
        compiler_params=_cparams("parallel", "arbitrary"),
        name="norm_matmul",
    )(xa, xb, g.reshape(1, d), w)


def _attn_kernel(meta_ref, q_ref, kc_ref, kp_ref, kn_ref, vc_ref, vp_ref, vn_ref,
                 csc_ref, csp_ref, csn_ref, sink_ref, o_ref, q_scr, k_scr, v_scr, o_scr):
    n = pl.program_id(0)
    nb = ATT_BLOCK // WINDOW
    band = 3 * WINDOW
    k_lo = jnp.where(meta_ref[1, n] == 1, WINDOW, 0)
    k_hi = jnp.where(meta_ref[2, n] == 1, 2 * WINDOW, band)

    def rope(x, cs):
        return x * cs[:, :HEAD_DIM] + pltpu.roll(x, HEAD_DIM // 2, 1) * cs[:, HEAD_DIM:]

    cs_c = csc_ref[...]
    for hq in range(N_Q_HEADS):
        sl = slice(hq * HEAD_DIM, (hq + 1) * HEAD_DIM)
        q_scr[hq] = rope(q_ref[:, sl].astype(F32), cs_c).astype(BF16)
    cs_p = csp_ref[...]
    cs_n = csn_ref[...]
    for h in range(N_KV_HEADS):
        sl = slice(h * HEAD_DIM, (h + 1) * HEAD_DIM)
        k_scr[h, 0:WINDOW, :] = rope(kp_ref[:, sl].astype(F32), cs_p).astype(BF16)
        k_scr[h, WINDOW:WINDOW + ATT_BLOCK, :] = rope(kc_ref[:, sl].astype(F32), cs_c).astype(BF16)
        k_scr[h, WINDOW + ATT_BLOCK:, :] = rope(kn_ref[:, sl].astype(F32), cs_n).astype(BF16)
        v_scr[h, 0:WINDOW, :] = vp_ref[:, sl]
        v_scr[h, WINDOW:WINDOW + ATT_BLOCK, :] = vc_ref[:, sl]
        v_scr[h, WINDOW + ATT_BLOCK:, :] = vn_ref[:, sl]

    qi = lax.broadcasted_iota(jnp.int32, (WINDOW, band), 0)
    kc = lax.broadcasted_iota(jnp.int32, (WINDOW, band), 1)
    rel = jnp.abs(qi + WINDOW - kc) <= WINDOW
    scale = HEAD_DIM ** -0.5 * LOG2_E

    def band_step(r, carry):
        r0 = pl.multiple_of(r * WINDOW, WINDOW)
        lo = jnp.where(r == 0, k_lo, 0)
        hi = jnp.where(r == nb - 1, k_hi, band)
        bias = jnp.where(jnp.logical_and(rel, jnp.logical_and(kc >= lo, kc < hi)), 0.0, NEG_INF)
        for h0 in range(0, N_KV_HEADS, HEADS_PER_PASS):
            heads = range(h0, h0 + HEADS_PER_PASS)
            kbs = {h: k_scr[h, pl.ds(r0, band), :] for h in heads}
            vbs = {h: v_scr[h, pl.ds(r0, band), :] for h in heads}
            scores = {(h, g): _dot_nt(q_scr[Q_PER_KV * h + g, pl.ds(r0, WINDOW), :], kbs[h])
                      for h in heads for g in range(Q_PER_KV)}
            for h in heads:
                for g in range(Q_PER_KV):
                    hq = Q_PER_KV * h + g
                    s = scores[(h, g)] * scale + bias
                    sk = sink_ref[hq]
                    m = jnp.maximum(jnp.max(s, axis=-1, keepdims=True), sk)
                    e = jnp.concatenate([jnp.exp2(s[:, j * WINDOW:(j + 1) * WINDOW] - m) for j in range(3)],
                                        axis=-1)
                    den = jnp.sum(e, axis=-1, keepdims=True) + jnp.exp2(sk - m)
                    o = jnp.dot(e.astype(BF16), vbs[h], preferred_element_type=F32) * (1.0 / den)
                    o_scr[hq, pl.ds(r0, WINDOW), :] = o.astype(BF16)
        return carry

    lax.fori_loop(0, nb, band_step, 0)
    for hq in range(N_Q_HEADS):
        o_ref[:, hq * HEAD_DIM:(hq + 1) * HEAD_DIM] = o_scr[hq]


def banded_attention(proj, meta, cs_table, sink_col):
    t = proj.shape[0]
    nblk = t // ATT_BLOCK
    sub = ATT_BLOCK // WINDOW
    n_small = t // WINDOW
    n_pos_small = cs_table.shape[0] // WINDOW
    grid_spec = pltpu.PrefetchScalarGridSpec(
        num_scalar_prefetch=1,
        grid=(nblk,),
        in_specs=[
            pl.BlockSpec((ATT_BLOCK, ATT_WIDTH), lambda n, m: (n, COL_Q)),
            pl.BlockSpec((ATT_BLOCK, KV_WIDTH), lambda n, m: (n, COL_K)),
            pl.BlockSpec((WINDOW, KV_WIDTH), lambda n, m: (jnp.maximum(n * sub - 1, 0), COL_K)),
            pl.BlockSpec((WINDOW, KV_WIDTH), lambda n, m: (jnp.minimum(n * sub + sub, n_small - 1), COL_K)),
            pl.BlockSpec((ATT_BLOCK, KV_WIDTH), lambda n, m: (n, COL_V)),
            pl.BlockSpec((WINDOW, KV_WIDTH), lambda n, m: (jnp.maximum(n * sub - 1, 0), COL_V)),
            pl.BlockSpec((WINDOW, KV_WIDTH), lambda n, m: (jnp.minimum(n * sub + sub, n_small - 1), COL_V)),
            pl.BlockSpec((ATT_BLOCK, 2 * HEAD_DIM), lambda n, m: (m[0, n], 0)),
            pl.BlockSpec((WINDOW, 2 * HEAD_DIM), lambda n, m: (jnp.maximum(m[0, n] * sub - 1, 0), 0)),
            pl.BlockSpec((WINDOW, 2 * HEAD_DIM),
                         lambda n, m: (jnp.minimum(m[0, n] * sub + sub, n_pos_small - 1), 0)),
            pl.BlockSpec((N_Q_HEADS, WINDOW, SINK_LANES), lambda n, m: (0, 0, 0)),
        ],
        out_specs=pl.BlockSpec((ATT_BLOCK, ATT_WIDTH), lambda n, m: (n, 0)),
        scratch_shapes=[
            pltpu.VMEM((N_Q_HEADS, ATT_BLOCK, HEAD_DIM), BF16),
            pltpu.VMEM((N_KV_HEADS, ATT_BLOCK + 2 * WINDOW, HEAD_DIM), BF16),
            pltpu.VMEM((N_KV_HEADS, ATT_BLOCK + 2 * WINDOW, HEAD_DIM), BF16),
            pltpu.VMEM((N_Q_HEADS, ATT_BLOCK, HEAD_DIM), BF16),
        ],
    )
    return pl.pallas_call(
        _attn_kernel,
        grid_spec=grid_spec,
        out_shape=jax.ShapeDtypeStruct((t, ATT_WIDTH), BF16),
        compiler_params=_cparams("parallel"),
        name="banded_attention",
    )(meta, proj, proj, proj, proj, proj, proj, proj, cs_table, cs_table, cs_table, sink_col)


def _gmlp_kernel(gu_ref, gz_ref, lng_ref, lnb_ref, ws_ref, bs_ref, o_ref, zn_scr, u_scr):
    z = jax.nn.gelu(gz_ref[...]).astype(F32)
    mu = jnp.mean(z, axis=-1, keepdims=True)
    zc = z - mu
    var = jnp.mean(zc * zc, axis=-1, keepdims=True)
    zn_scr[...] = (zc * lax.rsqrt(var + NORM_EPS) * lng_ref[...] + lnb_ref[...]).astype(BF16)
    u_scr[...] = jax.nn.gelu(gu_ref[...])
    gd = GM_WIDTH // GM_GROUPS
    for c in range(o_ref.shape[0] // GM_CHUNK):
        rs = slice(c * GM_CHUNK, (c + 1) * GM_CHUNK)
        for g in range(GM_GROUPS):
            cs = slice(g * gd, (g + 1) * gd)
            mixed = jnp.dot(ws_ref[g], zn_scr[rs, cs], preferred_element_type=F32) + bs_ref[g]
            o_ref[rs, cs] = (u_scr[rs, cs].astype(F32) * mixed).astype(BF16)


def spatial_gating(proj, ln_g, ln_b, w_s, b_s_b, *, bm):
    t = proj.shape[0]
    return pl.pallas_call(
        _gmlp_kernel,
        grid=(t // bm,),
        in_specs=[
            pl.BlockSpec((bm, GM_WIDTH), lambda i: (i, COL_GU)),
            pl.BlockSpec((bm, GM_WIDTH), lambda i: (i, COL_GZ)),
            pl.BlockSpec((1, GM_WIDTH), lambda i: (0, 0)),
            pl.BlockSpec((1, GM_WIDTH), lambda i: (0, 0)),
            pl.BlockSpec((GM_GROUPS, GM_CHUNK, GM_CHUNK), lambda i: (0, 0, 0)),
            pl.BlockSpec((GM_GROUPS, GM_CHUNK, GM_WIDTH // GM_GROUPS), lambda i: (0, 0, 0)),
        ],
        out_specs=pl.BlockSpec((bm, GM_WIDTH), lambda i: (i, 0)),
        out_shape=jax.ShapeDtypeStruct((t, GM_WIDTH), BF16),
        scratch_shapes=[pltpu.VMEM((bm, GM_WIDTH), BF16), pltpu.VMEM((bm, GM_WIDTH), BF16)],
        compiler_params=_cparams("parallel"),
        name="spatial_gating",
    )(proj, proj, ln_g.reshape(1, -1), ln_b.reshape(1, -1), w_s, b_s_b)


def _sigmoid(x):
    return 1.0 / (1.0 + jnp.exp(-x))


def _merge_kernel(att_ref, gm_ref, ga_ref, gb_ref, xa_ref, xb_ref, wa_ref, wg_ref, wo_ref, y_ref, *, na):
    i = pl.program_id(0)

    @pl.when(i < na)
    def _():
        y_ref[...] = xa_ref[...]

    @pl.when(i >= na)
    def _():
        y_ref[...] = xb_ref[...]

    a = jnp.dot(att_ref[...], wa_ref[...], preferred_element_type=F32)
    b = jnp.dot(gm_ref[...], wg_ref[...], preferred_element_type=F32)
    merged = _sigmoid(ga_ref[...].astype(F32)) * a + _sigmoid(gb_ref[...].astype(F32)) * b
    y_ref[...] += jnp.dot(merged.astype(BF16), wo_ref[...], preferred_element_type=F32)


def merge_project(att, gm, proj, xs, wa, wg, wo, *, bm):
    xa, xb, na, t = _row_parts(xs, bm)
    d = xa.shape[1]
    const = dict(pipeline_mode=pl.Buffered(1))
    return pl.pallas_call(
        functools.partial(_merge_kernel, na=na),
        grid=(t // bm,),
        in_specs=[
            pl.BlockSpec((bm, ATT_WIDTH), lambda i: (i, 0)),
            pl.BlockSpec((bm, GM_WIDTH), lambda i: (i, 0)),
            pl.BlockSpec((bm, d), lambda i: (i, COL_GA)),
            pl.BlockSpec((bm, d), lambda i: (i, COL_GB)),
        ] + _part_specs(bm, d, na, 1) + [
            pl.BlockSpec((ATT_WIDTH, d), lambda i: (0, 0), **const),
            pl.BlockSpec((GM_WIDTH, d), lambda i: (0, 0), **const),
            pl.BlockSpec((d, d), lambda i: (0, 0), **const),
        ],
        out_specs=pl.BlockSpec((bm, d), lambda i: (i, 0)),
        out_shape=jax.ShapeDtypeStruct((t, d), F32),
        compiler_params=_cparams("parallel"),
        name="merge_project",
    )(att, gm, proj, proj, xa, xb, wa, wg, wo)


def _cross_kernel(meta_ref, x_ref, g_ref, wq_ref, kv_ref, wo_ref, y_ref):
    x = x_ref[...]
    xn = _rms(x, g_ref[...]).astype(BF16)
    q = jnp.dot(xn, wq_ref[...], preferred_element_type=F32).astype(BF16)
    scale = MEM_HEAD_DIM ** -0.5 * LOG2_E
    heads = [slice(h * MEM_HEAD_DIM, (h + 1) * MEM_HEAD_DIM) for h in range(MEM_HEADS)]
    scores = [_dot_nt(q[:, sl], kv_ref[0, :, sl]) for sl in heads]
    outs = []
    for h, sl in enumerate(heads):
        vh = kv_ref[0, :, MEM_WIDTH + h * MEM_HEAD_DIM:MEM_WIDTH + (h + 1) * MEM_HEAD_DIM]
        s = scores[h] * scale
        m = jnp.max(s, axis=-1, keepdims=True)
        e = jnp.exp2(s - m)
        o_h = jnp.dot(e.astype(BF16), vh, preferred_element_type=F32) * (1.0 / jnp.sum(e, axis=-1, keepdims=True))
        outs.append(o_h.astype(BF16))
    o = jnp.concatenate(outs, axis=-1)
    y_ref[...] = x + jnp.dot(o, wo_ref[...], preferred_element_type=F32)


def cross_attention(x, meta, g, wq, kv, wo, *, bm):
    t, d = x.shape
    grid_spec = pltpu.PrefetchScalarGridSpec(
        num_scalar_prefetch=1,
        grid=(t // bm,),
        in_specs=[
            pl.BlockSpec((bm, d), lambda i, m: (i, 0)),
            pl.BlockSpec((1, d), lambda i, m: (0, 0)),
            pl.BlockSpec((d, MEM_WIDTH), lambda i, m: (0, 0)),
            pl.BlockSpec((1, MEM_TOKENS, 2 * MEM_WIDTH), lambda i, m: (m[3, i], 0, 0)),
            pl.BlockSpec((MEM_WIDTH, d), lambda i, m: (0, 0)),
        ],
        out_specs=pl.BlockSpec((bm, d), lambda i, m: (i, 0)),
    )
    return pl.pallas_call(
        _cross_kernel,
        grid_spec=grid_spec,
        out_shape=jax.ShapeDtypeStruct((t, d), F32),
        compiler_params=_cparams("parallel"),
        name="cross_attention",
    )(meta, x, g.reshape(1, d), wq, kv, wo)


N_TOP = PEER_TOPK + 1
CAND = [(a, b) for a in range(N_TOP) for b in range(N_TOP // (a + 1))]
CAND_ROWS = 8 * SUBLANES
assert len(CAND) <= CAND_ROWS
XPOSE_COLS = 256


def _oddeven_merge_sort(n):
    def merge(lo, hi, r):
        step = r * 2
        if step < hi - lo:
            yield from merge(lo, hi, step)
            yield from merge(lo + r, hi, step)
            yield from [(i, i + r) for i in range(lo + r, hi - r, step)]
        else:
            yield (lo, lo + r)

    def sort(lo, hi):
        if hi - lo >= 1:
            mid = lo + (hi - lo) // 2
            yield from sort(lo, mid)
            yield from sort(mid + 1, hi)
            yield from merge(lo, hi, 1)

    return list(sort(0, n - 1))


def _top_values(s, n_top, one_at_a_time=False):
    nblk = s.shape[0] // SUBLANES
    v = [s[d * SUBLANES:(d + 1) * SUBLANES, :] for d in range(nblk)]
    for a, b in _oddeven_merge_sort(nblk):
        v[a], v[b] = jnp.maximum(v[a], v[b]), jnp.minimum(v[a], v[b])
    sub = lax.broadcasted_iota(jnp.int32, v[0].shape, 0)
    vals = []
    for k in range(n_top):
        m = jnp.max(v[0], axis=0, keepdims=True)
        vals.append(m)
        remaining = n_top - 1 - k
        if remaining == 0:
            break
        hit = v[0] == m
        if one_at_a_time:
            first = jnp.min(jnp.where(hit, sub, SUBLANES), axis=0, keepdims=True)
            hit = sub == first
        for d in range(min(nblk, remaining)):
            v[d] = jnp.where(hit, v[d + 1] if d + 1 < nblk else NEG_INF, v[d])
    return vals


def _count_above(vals, x, strict):
    assert len(vals) == 16

    def above(v):
        return (v > x) if strict else (v >= x)

    bits = []

    def pivot(level, lo, size, depth=0):
        mid = lo + size // 2
        if depth == level:
            return vals[mid]
        return jnp.where(bits[depth], pivot(level, mid + 1, size // 2, depth + 1),
                         pivot(level, lo, size // 2, depth + 1))

    count = jnp.where(above(vals[15]), 1.0, 0.0)
    for level in range(4):
        bits.append(above(pivot(level, 0, 15)))
        count = count + jnp.where(bits[level], float(8 >> level), 0.0)
    return count


def _router_kernel(x_ref, g_ref, wqt_ref, k1_ref, k2_ref, xnt_ref, r2_ref, bz_ref, cnt_ref, a_ref,
                   qt_scr, c_scr):
    d = x_ref.shape[1]
    x = x_ref[...]
    inv = lax.rsqrt(jnp.mean(x * x, axis=-1, keepdims=True) + NORM_EPS)
    for c0 in range(0, d, XPOSE_COLS):
        cols = slice(c0, c0 + XPOSE_COLS)
        xn_c = x_ref[:, cols] * inv * g_ref[:, cols]
        xnt_ref[cols, :] = xn_c.T.astype(BF16)
    qt_scr[...] = jnp.dot(wqt_ref[...], xnt_ref[...],
                          preferred_element_type=F32).astype(BF16)
    c_scr[...] = jnp.full(c_scr.shape, NEG_INF, F32)

    def head(h, carry):
        off = pl.multiple_of(h * PEER_QDIM, PEER_QDIM)
        s1 = jnp.dot(k1_ref[...], qt_scr[pl.ds(off, PEER_HALF), :], preferred_element_type=F32)
        s2 = jnp.dot(k2_ref[...], qt_scr[pl.ds(off + PEER_HALF, PEER_HALF), :],
                     preferred_element_type=F32)
        v1 = _top_values(s1, N_TOP)
        v2 = _top_values(s2, N_TOP)
        for r, (a, b) in enumerate(CAND):
            c_scr[r:r + 1, :] = v1[a] + v2[b]
        c0 = c_scr[...]
        csort = _top_values(c0, N_TOP, one_at_a_time=True)
        thr = 0.5 * (csort[PEER_TOPK - 1] + csort[PEER_TOPK])
        top = v1[0] + v2[0]
        z = jnp.sum(jnp.where(c0 >= thr, jnp.exp(c0 - top), 0.0), axis=0, keepdims=True)
        r2_ref[h] = _count_above(v2[:PEER_TOPK], s2, strict=True).astype(BF16)
        bz_ref[h] = (jnp.exp(s2 - v2[0]) / z).astype(BF16)
        cnt_ref[h] = _count_above(v2[:PEER_TOPK], thr - s1, strict=False)
        a_ref[h] = jnp.exp(s1 - v1[0])
        return carry

    lax.fori_loop(0, PEER_HEADS, head, 0)


def peer_router(x, g, wqt, k1, k2, *, bm):
    t, d = x.shape
    head_spec = pl.BlockSpec((PEER_HEADS, N_KEYS, bm), lambda i: (0, 0, i))
    return pl.pallas_call(
        _router_kernel,
        grid=(t // bm,),
        in_specs=[
            pl.BlockSpec((bm, d), lambda i: (i, 0)),
            pl.BlockSpec((1, d), lambda i: (0, 0)),
            pl.BlockSpec((PEER_HEADS * PEER_QDIM, d), lambda i: (0, 0)),
            pl.BlockSpec((N_KEYS, PEER_HALF), lambda i: (0, 0)),
            pl.BlockSpec((N_KEYS, PEER_HALF), lambda i: (0, 0)),
        ],
        out_specs=[pl.BlockSpec((d, bm), lambda i: (0, i)), head_spec, head_spec, head_spec, head_spec],
        out_shape=[jax.ShapeDtypeStruct((d, t), BF16),
                   jax.ShapeDtypeStruct((PEER_HEADS, N_KEYS, t), BF16),
                   jax.ShapeDtypeStruct((PEER_HEADS, N_KEYS, t), BF16),
                   jax.ShapeDtypeStruct((PEER_HEADS, N_KEYS, t), F32),
                   jax.ShapeDtypeStruct((PEER_HEADS, N_KEYS, t), F32)],
        scratch_shapes=[pltpu.VMEM((PEER_HEADS * PEER_QDIM, bm), BF16), pltpu.VMEM((CAND_ROWS, bm), F32)],
        compiler_params=_cparams("parallel"),
        name="peer_router",
    )(x, g.reshape(1, d), wqt, k1, k2)


EXPERT_TILE = 2048
EXPERT_VMEM_LIMIT = 62 * 1024 * 1024
EXPERT_CHUNK = 256
KEYS_PER_TILE = EXPERT_TILE // N_KEYS
KEYS_PER_CHUNK = EXPERT_CHUNK // N_KEYS
CHUNKS_PER_TILE = EXPERT_TILE // EXPERT_CHUNK
TOKEN_GROUP = 2 * LANES
assert KEYS_PER_TILE % SUBLANES == 0


def _expert_kernel(xnt_ref, u_ref, vt_ref, r2_ref, bz_ref, cnt_ref, a_ref, x_hbm, gf_ref, y_ref, p_scr,
                   acc_scr, x_sem, *, tile_off, final_norm):
    e = pl.program_id(1)
    bm = y_ref.shape[0]

    def x_copy():
        row0 = pl.multiple_of((pl.program_id(0) + tile_off) * bm, bm)
        return pltpu.make_async_copy(x_hbm.at[pl.ds(row0, bm), :], y_ref, x_sem)

    @pl.when(e == 0)
    def _():
        acc_scr[...] = jnp.zeros(acc_scr.shape, F32)
        x_copy().start()

    for c in range(CHUNKS_PER_TILE):
        crow = slice(c * EXPERT_CHUNK, (c + 1) * EXPERT_CHUNK)
        h = jnp.dot(u_ref[crow, :], xnt_ref[...], preferred_element_type=F32)
        for kk in range(KEYS_PER_CHUNK):
            il = c * KEYS_PER_CHUNK + kk
            rows = slice(il * N_KEYS, (il + 1) * N_KEYS)
            for tg in range(bm // TOKEN_GROUP):
                ts = slice(tg * TOKEN_GROUP, (tg + 1) * TOKEN_GROUP)
                g = jax.nn.gelu(h[kk * N_KEYS:(kk + 1) * N_KEYS, ts].astype(BF16))
                p = jnp.zeros((N_KEYS, TOKEN_GROUP), BF16)
                for hd in range(PEER_HEADS):
                    cb = jnp.broadcast_to(cnt_ref[hd, il:il + 1, ts], (N_KEYS, TOKEN_GROUP)).astype(BF16)
                    ab = jnp.broadcast_to(a_ref[hd, il:il + 1, ts], (N_KEYS, TOKEN_GROUP)).astype(BF16)
                    p = p + jnp.where(r2_ref[hd, :, ts] < cb, bz_ref[hd, :, ts], 0.0) * ab
                p_scr[rows, ts] = p * g
    acc_scr[...] += jnp.dot(vt_ref[...], p_scr[...], preferred_element_type=F32)

    @pl.when(e == pl.num_programs(1) - 1)
    def _():
        x_copy().wait()
        for c0 in range(0, y_ref.shape[1], XPOSE_COLS):
            cols = slice(c0, c0 + XPOSE_COLS)
            y_ref[:, cols] += acc_scr[cols, :].T
        if final_norm:
            y_ref[...] = _rms(y_ref[...], gf_ref[...])


def peer_experts(xnt, u, vt, r2, bz, cnt, a, x, g_final, *, layer, bm, start=0, rows=None, final_norm=False):
    d = x.shape[1]
    rows = x.shape[0] if rows is None else rows
    assert bm % TOKEN_GROUP == 0 and start % bm == 0 and rows % bm == 0
    off = start // bm
    head_full = pl.BlockSpec((PEER_HEADS, N_KEYS, bm), lambda i, e: (0, 0, i + off))
    head_rows = pl.BlockSpec((PEER_HEADS, KEYS_PER_TILE, bm), lambda i, e: (0, e, i + off))
    return pl.pallas_call(
        functools.partial(_expert_kernel, tile_off=off, final_norm=final_norm),
        grid=(rows // bm, u.shape[1] // EXPERT_TILE),
        in_specs=[
            pl.BlockSpec((d, bm), lambda i, e: (0, i + off)),
            pl.BlockSpec((None, EXPERT_TILE, d), lambda i, e: (layer, e, 0)),
            pl.BlockSpec((None, d, EXPERT_TILE), lambda i, e: (layer, 0, e)),
            head_full, head_full, head_rows, head_rows,
            pl.BlockSpec(memory_space=pl.ANY),
            pl.BlockSpec((1, d), lambda i, e: (0, 0)),
        ],
        out_specs=pl.BlockSpec((bm, d), lambda i, e: (i, 0)),
        out_shape=jax.ShapeDtypeStruct((rows, d), F32),
        scratch_shapes=[
            pltpu.VMEM((EXPERT_TILE, bm), BF16),
            pltpu.VMEM((d, bm), F32),
            pltpu.SemaphoreType.DMA(()),
        ],
        compiler_params=pltpu.CompilerParams(dimension_semantics=("parallel", "arbitrary"),
                                             vmem_limit_bytes=EXPERT_VMEM_LIMIT),
        name="peer_experts",
    )(xnt, u, vt, r2, bz, cnt, a, x, g_final.reshape(1, d))


def _block_meta(groups, blk):
    pos, first, last, batch = [], [], [], []
    b0 = 0
    for (b, s) in groups:
        per = s // blk
        for bi in range(b):
            for j in range(per):
                pos.append(j)
                first.append(int(j == 0))
                last.append(int(j == per - 1))
                batch.append(b0 + bi)
        b0 += b
    return jnp.asarray(np.array([pos, first, last, batch], dtype=np.int32))


def _rope_table(max_pos):
    half = HEAD_DIM // 2
    inv = ROPE_THETA ** (-jnp.arange(half, dtype=F32) / half)
    ang = jnp.arange(max_pos, dtype=F32)[:, None] * inv[None, :]
    cos, sin = jnp.cos(ang), jnp.sin(ang)
    return jnp.concatenate([cos, cos, -sin, sin], axis=-1)


def _pick(t, pref):
    while t % pref:
        pref //= 2
    return pref


def _trunk(xs, mems, mix_norm, w_in, attn_sink, gm_ln_g, gm_ln_b, gm_w_s, gm_b_s, w_att_proj, w_gm_proj,
           w_out, cross_norm, mem_norm, w_q_mem, w_kv_mem, w_o_mem, peer_norm, w_q_peer, peer_k1, peer_k2,
           expert_u, expert_v, final_norm):
    groups = [(x.shape[0], x.shape[1]) for x in xs]
    depth = w_in.shape[0]
    for (_, s) in groups:
        assert s % ATT_BLOCK == 0
    x_parts = tuple(x.reshape(-1, D_MODEL) for x in xs)
    if len(x_parts) > 2:
        x_parts = (jnp.concatenate(x_parts, axis=0),)
    mem = jnp.concatenate([m.reshape(-1, D_MODEL) for m in mems], axis=0)
    n_mem_rows = mem.shape[0]
    meta = _block_meta(groups, ATT_BLOCK)
    cs_table = _rope_table(max(s for _, s in groups))

    common = int(np.gcd.reduce([b * s for b, s in groups]))
    bm_big = _pick(common, 1024)
    bm_mid = _pick(common, 512)
    bm_small = _pick(common, 256)
    u_all = expert_u.astype(BF16)
    vt_all = jnp.swapaxes(expert_v, 1, 2).astype(BF16)
    in_splits = np.cumsum([ATT_WIDTH, KV_WIDTH, KV_WIDTH, GM_WIDTH, GM_WIDTH, D_MODEL]).tolist()

    for l in range(depth):
        q_w, k_w, v_w, gu_w, gz_w, ga_w, gb_w = jnp.split(w_in[l], in_splits, axis=1)
        w_in_p = jnp.concatenate([q_w, gu_w, gz_w, ga_w, gb_w, k_w, v_w], axis=1).astype(BF16)
        sink_col = jnp.broadcast_to((attn_sink[l].astype(F32) * LOG2_E)[:, None, None],
                                    (N_Q_HEADS, WINDOW, SINK_LANES))
        b_s_b = jnp.broadcast_to(gm_b_s[l].astype(F32)[:, :, None],
                                 (GM_GROUPS, GM_CHUNK, GM_WIDTH // GM_GROUPS))

        proj = norm_matmul(x_parts, mix_norm[l], w_in_p, bm=bm_big, bn=1024)
        att = banded_attention(proj, meta, cs_table, sink_col)
        gm = spatial_gating(proj, gm_ln_g[l], gm_ln_b[l], gm_w_s[l].astype(BF16), b_s_b, bm=ATT_BLOCK)
        x = merge_project(att, gm, proj, x_parts, w_att_proj[l].astype(BF16), w_gm_proj[l].astype(BF16),
                          w_out[l].astype(BF16), bm=bm_small)

        kv = norm_matmul((mem,), mem_norm[l], w_kv_mem[l].astype(BF16), bm=_pick(n_mem_rows, 512), bn=1024)
        kv = kv.reshape(-1, MEM_TOKENS, 2 * MEM_WIDTH)
        x = cross_attention(x, meta, cross_norm[l], w_q_mem[l].astype(BF16), kv, w_o_mem[l].astype(BF16),
                            bm=ATT_BLOCK)

        xnt, r2, bz, cnt, a = peer_router(x, peer_norm[l], w_q_peer[l].T.astype(BF16),
                                          peer_k1[l].astype(BF16), peer_k2[l].astype(BF16), bm=bm_mid)
        experts = functools.partial(peer_experts, xnt, u_all, vt_all, r2, bz, cnt, a, x, final_norm,
                                    layer=l, bm=bm_mid)
        if l + 1 < depth:
            x_parts = (experts(),)
    outs = []
    start = 0
    for (b, s) in groups:
        y = experts(start=start, rows=b * s, final_norm=True)
        outs.append(y.reshape(b, s, D_MODEL))
        start += b * s
    return tuple(outs)


def kernel(x_prompt, x_sample, mem_prompt, mem_sample, mix_norm, w_in, attn_sink, gm_ln_g, gm_ln_b, gm_w_s, gm_b_s, w_att_proj, w_gm_proj, w_out, cross_norm, mem_norm, w_q_mem, w_kv_mem, w_o_mem, peer_norm, w_q_peer, peer_k1, peer_k2, expert_u, expert_v, final_norm):
    return _trunk((x_prompt, x_sample), (mem_prompt, mem_sample), mix_norm, w_in, attn_sink, gm_ln_g,
                  gm_ln_b, gm_w_s, gm_b_s, w_att_proj, w_gm_proj, w_out, cross_norm, mem_norm, w_q_mem,
                  w_kv_mem, w_o_mem, peer_norm, w_q_peer, peer_k1, peer_k2, expert_u, expert_v, final_norm)
```

```python
import functools

import numpy as np
import jax
import jax.numpy as jnp
from jax import lax
from jax.experimental import pallas as pl
from jax.experimental.pallas import tpu as pltpu

F32 = jnp.float32
BF16 = jnp.bfloat16

D_MODEL = 2048
HEAD_DIM = 128
N_Q_HEADS = 16
N_KV_HEADS = 4
Q_PER_KV = N_Q_HEADS // N_KV_HEADS
ATT_WIDTH = N_Q_HEADS * HEAD_DIM
KV_WIDTH = N_KV_HEADS * HEAD_DIM
WINDOW = 128
ROPE_THETA = 10000.0
GM_WIDTH = 2048
GM_GROUPS = 16
GM_CHUNK = 128
MEM_TOKENS = 256
MEM_HEADS = 4
MEM_HEAD_DIM = 128
MEM_WIDTH = MEM_HEADS * MEM_HEAD_DIM
PEER_HEADS = 8
PEER_QDIM = 256
PEER_HALF = PEER_QDIM // 2
N_KEYS = 128
N_EXPERTS = N_KEYS * N_KEYS
PEER_TOPK = 16
NORM_EPS = 1e-6
NEG_INF = -1e30
LOG2_E = 1.4426950408889634

COL_Q, COL_GU, COL_GZ, COL_GA, COL_GB = 0, 1, 2, 3, 4
COL_K = (ATT_WIDTH + 4 * D_MODEL) // KV_WIDTH
COL_V = COL_K + 1

LANES = 128
SUBLANES = 8
ATT_BLOCK = 512
SINK_LANES = LANES
HEADS_PER_PASS = 2
VMEM_LIMIT = 56 * 1024 * 1024


def _cparams(*sem):
    return pltpu.CompilerParams(dimension_semantics=sem, vmem_limit_bytes=VMEM_LIMIT)


def _rms(x, g):
    return x * lax.rsqrt(jnp.mean(x * x, axis=-1, keepdims=True) + NORM_EPS) * g


def _dot_nt(a, b):
    return lax.dot_general(a, b, (((1,), (1,)), ((), ())), preferred_element_type=F32)


def _row_parts(xs, bm):
    xa, xb = (xs[0], xs[0]) if len(xs) == 1 else xs
    assert all(x.shape[0] % bm == 0 for x in xs)
    return xa, xb, xa.shape[0] // bm, sum(x.shape[0] for x in xs)


def _part_specs(bm, d, na, ndim_grid):
    if ndim_grid == 1:
        return [pl.BlockSpec((bm, d), lambda i: (jnp.minimum(i, na - 1), 0)),
                pl.BlockSpec((bm, d), lambda i: (jnp.maximum(i - na, 0), 0))]
    return [pl.BlockSpec((bm, d), lambda i, j: (jnp.minimum(i, na - 1), 0)),
            pl.BlockSpec((bm, d), lambda i, j: (jnp.maximum(i - na, 0), 0))]


def _norm_matmul_kernel(xa_ref, xb_ref, g_ref, w_ref, o_ref, xn_ref, *, na):
    i = pl.program_id(0)
    first = pl.program_id(1) == 0

    @pl.when(jnp.logical_and(first, i < na))
    def _():
        xn_ref[...] = _rms(xa_ref[...], g_ref[...]).astype(BF16)

    @pl.when(jnp.logical_and(first, i >= na))
    def _():
        xn_ref[...] = _rms(xb_ref[...], g_ref[...]).astype(BF16)

    o_ref[...] = jnp.dot(xn_ref[...], w_ref[...], preferred_element_type=F32).astype(o_ref.dtype)


def norm_matmul(xs, g, w, *, bm, bn):
    xa, xb, na, t = _row_parts(xs, bm)
    d = xa.shape[1]
    n = w.shape[1]
    assert n % bn == 0
    return pl.pallas_call(
        functools.partial(_norm_matmul_kernel, na=na),
        grid=(t // bm, n // bn),
        in_specs=_part_specs(bm, d, na, 2) + [
            pl.BlockSpec((1, d), lambda i, j: (0, 0)),
            pl.BlockSpec((d, bn), lambda i, j: (0, j)),
        ],
        out_specs=pl.BlockSpec((bm, bn), lambda i, j: (i, j)),
        out_shape=jax.ShapeDtypeStruct((t, n), BF16),
        scratch_shapes=[pltpu.VMEM((bm, d), BF16)],
        compiler_params=_cparams("parallel", "arbitrary"),
        name="norm_matmul",
    )(xa, xb, g.reshape(1, d), w)


def _attn_kernel(meta_ref, q_ref, kc_ref, kp_ref, kn_ref, vc_ref, vp_ref, vn_ref,
                 csc_ref, csp_ref, csn_ref, sink_ref, o_ref, q_scr, k_scr, v_scr, o_scr):
    n = pl.program_id(0)
    nb = ATT_BLOCK // WINDOW
    band = 3 * WINDOW
    k_lo = jnp.where(meta_ref[1, n] == 1, WINDOW, 0)
    k_hi = jnp.where(meta_ref[2, n] == 1, 2 * WINDOW, band)

    def rope(x, cs):
        return x * cs[:, :HEAD_DIM] + pltpu.roll(x, HEAD_DIM // 2, 1) * cs[:, HEAD_DIM:]

    cs_c = csc_ref[...]
    for hq in range(N_Q_HEADS):
        sl = slice(hq * HEAD_DIM, (hq + 1) * HEAD_DIM)
        q_scr[hq] = rope(q_ref[:, sl].astype(F32), cs_c).astype(BF16)
    cs_p = csp_ref[...]
    cs_n = csn_ref[...]
    for h in range(N_KV_HEADS):
        sl = slice(h * HEAD_DIM, (h + 1) * HEAD_DIM)
        k_scr[h, 0:WINDOW, :] = rope(kp_ref[:, sl].astype(F32), cs_p).astype(BF16)
        k_scr[h, WINDOW:WINDOW + ATT_BLOCK, :] = rope(kc_ref[:, sl].astype(F32), cs_c).astype(BF16)
        k_scr[h, WINDOW + ATT_BLOCK:, :] = rope(kn_ref[:, sl].astype(F32), cs_n).astype(BF16)
        v_scr[h, 0:WINDOW, :] = vp_ref[:, sl]
        v_scr[h, WINDOW:WINDOW + ATT_BLOCK, :] = vc_ref[:, sl]
        v_scr[h, WINDOW + ATT_BLOCK:, :] = vn_ref[:, sl]

    qi = lax.broadcasted_iota(jnp.int32, (WINDOW, band), 0)
    kc = lax.broadcasted_iota(jnp.int32, (WINDOW, band), 1)
    rel = jnp.abs(qi + WINDOW - kc) <= WINDOW
    scale = HEAD_DIM ** -0.5 * LOG2_E

    def band_step(r, carry):
        r0 = pl.multiple_of(r * WINDOW, WINDOW)
        lo = jnp.where(r == 0, k_lo, 0)
        hi = jnp.where(r == nb - 1, k_hi, band)
        bias = jnp.where(jnp.logical_and(rel, jnp.logical_and(kc >= lo, kc < hi)), 0.0, NEG_INF)
        for h0 in range(0, N_KV_HEADS, HEADS_PER_PASS):
            heads = range(h0, h0 + HEADS_PER_PASS)
            kbs = {h: k_scr[h, pl.ds(r0, band), :] for h in heads}
            vbs = {h: v_scr[h, pl.ds(r0, band), :] for h in heads}
            scores = {(h, g): _dot_nt(q_scr[Q_PER_KV * h + g, pl.ds(r0, WINDOW), :], kbs[h])
                      for h in heads for g in range(Q_PER_KV)}
            for h in heads:
                for g in range(Q_PER_KV):
                    hq = Q_PER_KV * h + g
                    s = scores[(h, g)] * scale + bias
                    sk = sink_ref[hq]
                    m = jnp.maximum(jnp.max(s, axis=-1, keepdims=True), sk)
                    e = jnp.concatenate([jnp.exp2(s[:, j * WINDOW:(j + 1) * WINDOW] - m) for j in range(3)],
                                        axis=-1)
                    den = jnp.sum(e, axis=-1, keepdims=True) + jnp.exp2(sk - m)
                    o = jnp.dot(e.astype(BF16), vbs[h], preferred_element_type=F32) * (1.0 / den)
                    o_scr[hq, pl.ds(r0, WINDOW), :] = o.astype(BF16)
        return carry

    lax.fori_loop(0, nb, band_step, 0)
    for hq in range(N_Q_HEADS):
        o_ref[:, hq * HEAD_DIM:(hq + 1) * HEAD_DIM] = o_scr[hq]


def banded_attention(proj, meta, cs_table, sink_col):
    t = proj.shape[0]
    nblk = t // ATT_BLOCK
    sub = ATT_BLOCK // WINDOW
    n_small = t // WINDOW
    n_pos_small = cs_table.shape[0] // WINDOW
    grid_spec = pltpu.PrefetchScalarGridSpec(
        num_scalar_prefetch=1,
        grid=(nblk,),
        in_specs=[
            pl.BlockSpec((ATT_BLOCK, ATT_WIDTH), lambda n, m: (n, COL_Q)),
            pl.BlockSpec((ATT_BLOCK, KV_WIDTH), lambda n, m: (n, COL_K)),
            pl.BlockSpec((WINDOW, KV_WIDTH), lambda n, m: (jnp.maximum(n * sub - 1, 0), COL_K)),
            pl.BlockSpec((WINDOW, KV_WIDTH), lambda n, m: (jnp.minimum(n * sub + sub, n_small - 1), COL_K)),
            pl.BlockSpec((ATT_BLOCK, KV_WIDTH), lambda n, m: (n, COL_V)),
            pl.BlockSpec((WINDOW, KV_WIDTH), lambda n, m: (jnp.maximum(n * sub - 1, 0), COL_V)),
            pl.BlockSpec((WINDOW, KV_WIDTH), lambda n, m: (jnp.minimum(n * sub + sub, n_small - 1), COL_V)),
            pl.BlockSpec((ATT_BLOCK, 2 * HEAD_DIM), lambda n, m: (m[0, n], 0)),
            pl.BlockSpec((WINDOW, 2 * HEAD_DIM), lambda n, m: (jnp.maximum(m[0, n] * sub - 1, 0), 0)),
            pl.BlockSpec((WINDOW, 2 * HEAD_DIM),
                         lambda n, m: (jnp.minimum(m[0, n] * sub + sub, n_pos_small - 1), 0)),
            pl.BlockSpec((N_Q_HEADS, WINDOW, SINK_LANES), lambda n, m: (0, 0, 0)),
        ],
        out_specs=pl.BlockSpec((ATT_BLOCK, ATT_WIDTH), lambda n, m: (n, 0)),
        scratch_shapes=[
            pltpu.VMEM((N_Q_HEADS, ATT_BLOCK, HEAD_DIM), BF16),
            pltpu.VMEM((N_KV_HEADS, ATT_BLOCK + 2 * WINDOW, HEAD_DIM), BF16),
            pltpu.VMEM((N_KV_HEADS, ATT_BLOCK + 2 * WINDOW, HEAD_DIM), BF16),
            pltpu.VMEM((N_Q_HEADS, ATT_BLOCK, HEAD_DIM), BF16),
        ],
    )
    return pl.pallas_call(
        _attn_kernel,
        grid_spec=grid_spec,
        out_shape=jax.ShapeDtypeStruct((t, ATT_WIDTH), BF16),
        compiler_params=_cparams("parallel"),
        name="banded_attention",
    )(meta, proj, proj, proj, proj, proj, proj, proj, cs_table, cs_table, cs_table, sink_col)


def _spatial_gating(gu_ref, gz_ref, lng_ref, lnb_ref, ws_ref, bs_ref, o_ref, zn_scr, u_scr):
    z = jax.nn.gelu(gz_ref[...]).astype(F32)
    mu = jnp.mean(z, axis=-1, keepdims=True)
    zc = z - mu
    var = jnp.mean(zc * zc, axis=-1, keepdims=True)
    zn_scr[...] = (zc * lax.rsqrt(var + NORM_EPS) * lng_ref[...] + lnb_ref[...]).astype(BF16)
    u_scr[...] = jax.nn.gelu(gu_ref[...])
    gd = GM_WIDTH // GM_GROUPS
    for c in range(o_ref.shape[0] // GM_CHUNK):
        rs = slice(c * GM_CHUNK, (c + 1) * GM_CHUNK)
        for g in range(GM_GROUPS):
            cs = slice(g * gd, (g + 1) * gd)
            mixed = jnp.dot(ws_ref[g], zn_scr[rs, cs], preferred_element_type=F32) + bs_ref[g]
            o_ref[rs, cs] = (u_scr[rs, cs].astype(F32) * mixed).astype(BF16)


def _sigmoid(x):
    return 1.0 / (1.0 + jnp.exp(-x))


def _merge_kernel(att_ref, gu_ref, gz_ref, ga_ref, gb_ref, xa_ref, xb_ref, lng_ref, lnb_ref, ws_ref, bs_ref,
                  wa_ref, wg_ref, wo_ref, y_ref, gm_scr, zn_scr, u_scr, *, na):
    i = pl.program_id(0)

    @pl.when(i < na)
    def _():
        y_ref[...] = xa_ref[...]

    @pl.when(i >= na)
    def _():
        y_ref[...] = xb_ref[...]

    a = jnp.dot(att_ref[...], wa_ref[...], preferred_element_type=F32)
    _spatial_gating(gu_ref, gz_ref, lng_ref, lnb_ref, ws_ref, bs_ref, gm_scr, zn_scr, u_scr)
    b = jnp.dot(gm_scr[...], wg_ref[...], preferred_element_type=F32)
    merged = _sigmoid(ga_ref[...].astype(F32)) * a + _sigmoid(gb_ref[...].astype(F32)) * b
    y_ref[...] += jnp.dot(merged.astype(BF16), wo_ref[...], preferred_element_type=F32)


def merge_project(att, proj, xs, ln_g, ln_b, w_s, b_s_b, wa, wg, wo, *, bm):
    xa, xb, na, t = _row_parts(xs, bm)
    d = xa.shape[1]
    assert bm % GM_CHUNK == 0
    const = dict(pipeline_mode=pl.Buffered(1))
    return pl.pallas_call(
        functools.partial(_merge_kernel, na=na),
        grid=(t // bm,),
        in_specs=[
            pl.BlockSpec((bm, ATT_WIDTH), lambda i: (i, 0)),
            pl.BlockSpec((bm, GM_WIDTH), lambda i: (i, COL_GU)),
            pl.BlockSpec((bm, GM_WIDTH), lambda i: (i, COL_GZ)),
            pl.BlockSpec((bm, d), lambda i: (i, COL_GA)),
            pl.BlockSpec((bm, d), lambda i: (i, COL_GB)),
        ] + _part_specs(bm, d, na, 1) + [
            pl.BlockSpec((1, GM_WIDTH), lambda i: (0, 0)),
            pl.BlockSpec((1, GM_WIDTH), lambda i: (0, 0)),
            pl.BlockSpec((GM_GROUPS, GM_CHUNK, GM_CHUNK), lambda i: (0, 0, 0)),
            pl.BlockSpec((GM_GROUPS, GM_CHUNK, GM_WIDTH // GM_GROUPS), lambda i: (0, 0, 0)),
            pl.BlockSpec((ATT_WIDTH, d), lambda i: (0, 0), **const),
            pl.BlockSpec((GM_WIDTH, d), lambda i: (0, 0), **const),
            pl.BlockSpec((d, d), lambda i: (0, 0), **const),
        ],
        out_specs=pl.BlockSpec((bm, d), lambda i: (i, 0)),
        out_shape=jax.ShapeDtypeStruct((t, d), F32),
        scratch_shapes=[pltpu.VMEM((bm, GM_WIDTH), BF16)] * 3,
        compiler_params=_cparams("parallel"),
        name="merge_project",
    )(att, proj, proj, proj, proj, xa, xb, ln_g.reshape(1, -1), ln_b.reshape(1, -1), w_s, b_s_b, wa, wg, wo)


def _cross_kernel(meta_ref, x_ref, g_ref, wq_ref, kv_ref, wo_ref, y_ref):
    x = x_ref[...]
    xn = _rms(x, g_ref[...]).astype(BF16)
    q = jnp.dot(xn, wq_ref[...], preferred_element_type=F32).astype(BF16)
    scale = MEM_HEAD_DIM ** -0.5 * LOG2_E
    heads = [slice(h * MEM_HEAD_DIM, (h + 1) * MEM_HEAD_DIM) for h in range(MEM_HEADS)]
    scores = [_dot_nt(q[:, sl], kv_ref[0, :, sl]) for sl in heads]
    outs = []
    for h, sl in enumerate(heads):
        vh = kv_ref[0, :, MEM_WIDTH + h * MEM_HEAD_DIM:MEM_WIDTH + (h + 1) * MEM_HEAD_DIM]
        s = scores[h] * scale
        m = jnp.max(s, axis=-1, keepdims=True)
        e = jnp.exp2(s - m)
        o_h = jnp.dot(e.astype(BF16), vh, preferred_element_type=F32) * (1.0 / jnp.sum(e, axis=-1, keepdims=True))
        outs.append(o_h.astype(BF16))
    o = jnp.concatenate(outs, axis=-1)
    y_ref[...] = x + jnp.dot(o, wo_ref[...], preferred_element_type=F32)


def cross_attention(x, meta, g, wq, kv, wo, *, bm):
    t, d = x.shape
    grid_spec = pltpu.PrefetchScalarGridSpec(
        num_scalar_prefetch=1,
        grid=(t // bm,),
        in_specs=[
            pl.BlockSpec((bm, d), lambda i, m: (i, 0)),
            pl.BlockSpec((1, d), lambda i, m: (0, 0)),
            pl.BlockSpec((d, MEM_WIDTH), lambda i, m: (0, 0)),
            pl.BlockSpec((1, MEM_TOKENS, 2 * MEM_WIDTH), lambda i, m: (m[3, i], 0, 0)),
            pl.BlockSpec((MEM_WIDTH, d), lambda i, m: (0, 0)),
        ],
        out_specs=pl.BlockSpec((bm, d), lambda i, m: (i, 0)),
    )
    return pl.pallas_call(
        _cross_kernel,
        grid_spec=grid_spec,
        out_shape=jax.ShapeDtypeStruct((t, d), F32),
        compiler_params=_cparams("parallel"),
        name="cross_attention",
    )(meta, x, g.reshape(1, d), wq, kv, wo)


N_TOP = PEER_TOPK + 1
CAND = [(a, b) for a in range(N_TOP) for b in range(N_TOP // (a + 1))]
CAND_ROWS = 8 * SUBLANES
assert len(CAND) <= CAND_ROWS
XPOSE_COLS = 256


def _oddeven_merge_sort(n):
    def merge(lo, hi, r):
        step = r * 2
        if step < hi - lo:
            yield from merge(lo, hi, step)
            yield from merge(lo + r, hi, step)
            yield from [(i, i + r) for i in range(lo + r, hi - r, step)]
        else:
            yield (lo, lo + r)

    def sort(lo, hi):
        if hi - lo >= 1:
            mid = lo + (hi - lo) // 2
            yield from sort(lo, mid)
            yield from sort(mid + 1, hi)
            yield from merge(lo, hi, 1)

    return list(sort(0, n - 1))


def _top_values(s, n_top, one_at_a_time=False):
    nblk = s.shape[0] // SUBLANES
    v = [s[d * SUBLANES:(d + 1) * SUBLANES, :] for d in range(nblk)]
    for a, b in _oddeven_merge_sort(nblk):
        v[a], v[b] = jnp.maximum(v[a], v[b]), jnp.minimum(v[a], v[b])
    sub = lax.broadcasted_iota(jnp.int32, v[0].shape, 0)
    vals = []
    for k in range(n_top):
        m = jnp.max(v[0], axis=0, keepdims=True)
        vals.append(m)
        remaining = n_top - 1 - k
        if remaining == 0:
            break
        hit = v[0] == m
        if one_at_a_time:
            first = jnp.min(jnp.where(hit, sub, SUBLANES), axis=0, keepdims=True)
            hit = sub == first
        for d in range(min(nblk, remaining)):
            v[d] = jnp.where(hit, v[d + 1] if d + 1 < nblk else NEG_INF, v[d])
    return vals


def _count_above(vals, x, strict):
    assert len(vals) == 16

    def above(v):
        return (v > x) if strict else (v >= x)

    bits = []

    def pivot(level, lo, size, depth=0):
        mid = lo + size // 2
        if depth == level:
            return vals[mid]
        return jnp.where(bits[depth], pivot(level, mid + 1, size // 2, depth + 1),
                         pivot(level, lo, size // 2, depth + 1))

    count = jnp.where(above(vals[15]), 1.0, 0.0)
    for level in range(4):
        bits.append(above(pivot(level, 0, 15)))
        count = count + jnp.where(bits[level], float(8 >> level), 0.0)
    return count


def _router_kernel(x_ref, g_ref, wqt_ref, k1_ref, k2_ref, xnt_ref, r2_ref, bz_ref, cnt_ref, a_ref,
                   qt_scr, c_scr):
    d = x_ref.shape[1]
    x = x_ref[...]
    inv = lax.rsqrt(jnp.mean(x * x, axis=-1, keepdims=True) + NORM_EPS)
    for c0 in range(0, d, XPOSE_COLS):
        cols = slice(c0, c0 + XPOSE_COLS)
        xn_c = x_ref[:, cols] * inv * g_ref[:, cols]
        xnt_ref[cols, :] = xn_c.T.astype(BF16)
    qt_scr[...] = jnp.dot(wqt_ref[...], xnt_ref[...],
                          preferred_element_type=F32).astype(BF16)
    c_scr[...] = jnp.full(c_scr.shape, NEG_INF, F32)

    def head(h, carry):
        off = pl.multiple_of(h * PEER_QDIM, PEER_QDIM)
        s1 = jnp.dot(k1_ref[...], qt_scr[pl.ds(off, PEER_HALF), :], preferred_element_type=F32)
        s2 = jnp.dot(k2_ref[...], qt_scr[pl.ds(off + PEER_HALF, PEER_HALF), :],
                     preferred_element_type=F32)
        v1 = _top_values(s1, N_TOP)
        v2 = _top_values(s2, N_TOP)
        for r, (a, b) in enumerate(CAND):
            c_scr[r:r + 1, :] = v1[a] + v2[b]
        c0 = c_scr[...]
        csort = _top_values(c0, N_TOP, one_at_a_time=True)
        thr = 0.5 * (csort[PEER_TOPK - 1] + csort[PEER_TOPK])
        top = v1[0] + v2[0]
        z = jnp.sum(jnp.where(c0 >= thr, jnp.exp(c0 - top), 0.0), axis=0, keepdims=True)
        r2_ref[h] = _count_above(v2[:PEER_TOPK], s2, strict=True).astype(BF16)
        bz_ref[h] = (jnp.exp(s2 - v2[0]) / z).astype(BF16)
        cnt_ref[h] = _count_above(v2[:PEER_TOPK], thr - s1, strict=False)
        a_ref[h] = jnp.exp(s1 - v1[0])
        return carry

    lax.fori_loop(0, PEER_HEADS, head, 0)


def peer_router(x, g, wqt, k1, k2, *, bm):
    t, d = x.shape
    head_spec = pl.BlockSpec((PEER_HEADS, N_KEYS, bm), lambda i: (0, 0, i))
    return pl.pallas_call(
        _router_kernel,
        grid=(t // bm,),
        in_specs=[
            pl.BlockSpec((bm, d), lambda i: (i, 0)),
            pl.BlockSpec((1, d), lambda i: (0, 0)),
            pl.BlockSpec((PEER_HEADS * PEER_QDIM, d), lambda i: (0, 0)),
            pl.BlockSpec((N_KEYS, PEER_HALF), lambda i: (0, 0)),
            pl.BlockSpec((N_KEYS, PEER_HALF), lambda i: (0, 0)),
        ],
        out_specs=[pl.BlockSpec((d, bm), lambda i: (0, i)), head_spec, head_spec, head_spec, head_spec],
        out_shape=[jax.ShapeDtypeStruct((d, t), BF16),
                   jax.ShapeDtypeStruct((PEER_HEADS, N_KEYS, t), BF16),
                   jax.ShapeDtypeStruct((PEER_HEADS, N_KEYS, t), BF16),
                   jax.ShapeDtypeStruct((PEER_HEADS, N_KEYS, t), F32),
                   jax.ShapeDtypeStruct((PEER_HEADS, N_KEYS, t), F32)],
        scratch_shapes=[pltpu.VMEM((PEER_HEADS * PEER_QDIM, bm), BF16), pltpu.VMEM((CAND_ROWS, bm), F32)],
        compiler_params=_cparams("parallel"),
        name="peer_router",
    )(x, g.reshape(1, d), wqt, k1, k2)


EXPERT_TILE = 2048
EXPERT_VMEM_LIMIT = 62 * 1024 * 1024
EXPERT_CHUNK = 256
KEYS_PER_TILE = EXPERT_TILE // N_KEYS
KEYS_PER_CHUNK = EXPERT_CHUNK // N_KEYS
CHUNKS_PER_TILE = EXPERT_TILE // EXPERT_CHUNK
TOKEN_GROUP = 2 * LANES
assert KEYS_PER_TILE % SUBLANES == 0


def _expert_kernel(xnt_ref, u_ref, vt_ref, r2_ref, bz_ref, cnt_ref, a_ref, x_hbm, gf_ref, y_ref, p_scr,
                   acc_scr, x_sem, *, tile_off, final_norm):
    e = pl.program_id(1)
    bm = y_ref.shape[0]

    def x_copy():
        row0 = pl.multiple_of((pl.program_id(0) + tile_off) * bm, bm)
        return pltpu.make_async_copy(x_hbm.at[pl.ds(row0, bm), :], y_ref, x_sem)

    @pl.when(e == 0)
    def _():
        acc_scr[...] = jnp.zeros(acc_scr.shape, F32)
        x_copy().start()

    for c in range(CHUNKS_PER_TILE):
        crow = slice(c * EXPERT_CHUNK, (c + 1) * EXPERT_CHUNK)
        h = jnp.dot(u_ref[crow, :], xnt_ref[...], preferred_element_type=F32)
        for kk in range(KEYS_PER_CHUNK):
            il = c * KEYS_PER_CHUNK + kk
            rows = slice(il * N_KEYS, (il + 1) * N_KEYS)
            for tg in range(bm // TOKEN_GROUP):
                ts = slice(tg * TOKEN_GROUP, (tg + 1) * TOKEN_GROUP)
                g = jax.nn.gelu(h[kk * N_KEYS:(kk + 1) * N_KEYS, ts].astype(BF16))
                p = jnp.zeros((N_KEYS, TOKEN_GROUP), BF16)
                for hd in range(PEER_HEADS):
                    cb = jnp.broadcast_to(cnt_ref[hd, il:il + 1, ts], (N_KEYS, TOKEN_GROUP)).astype(BF16)
                    ab = jnp.broadcast_to(a_ref[hd, il:il + 1, ts], (N_KEYS, TOKEN_GROUP)).astype(BF16)
                    p = p + jnp.where(r2_ref[hd, :, ts] < cb, bz_ref[hd, :, ts], 0.0) * ab
                p_scr[rows, ts] = p * g
    acc_scr[...] += jnp.dot(vt_ref[...], p_scr[...], preferred_element_type=F32)

    @pl.when(e == pl.num_programs(1) - 1)
    def _():
        x_copy().wait()
        for c0 in range(0, y_ref.shape[1], XPOSE_COLS):
            cols = slice(c0, c0 + XPOSE_COLS)
            y_ref[:, cols] += acc_scr[cols, :].T
        if final_norm:
            y_ref[...] = _rms(y_ref[...], gf_ref[...])


def peer_experts(xnt, u, vt, r2, bz, cnt, a, x, g_final, *, layer, bm, start=0, rows=None, final_norm=False):
    d = x.shape[1]
    rows = x.shape[0] if rows is None else rows
    assert bm % TOKEN_GROUP == 0 and start % bm == 0 and rows % bm == 0
    off = start // bm
    head_full = pl.BlockSpec((PEER_HEADS, N_KEYS, bm), lambda i, e: (0, 0, i + off))
    head_rows = pl.BlockSpec((PEER_HEADS, KEYS_PER_TILE, bm), lambda i, e: (0, e, i + off))
    return pl.pallas_call(
        functools.partial(_expert_kernel, tile_off=off, final_norm=final_norm),
        grid=(rows // bm, u.shape[1] // EXPERT_TILE),
        in_specs=[
            pl.BlockSpec((d, bm), lambda i, e: (0, i + off)),
            pl.BlockSpec((None, EXPERT_TILE, d), lambda i, e: (layer, e, 0)),
            pl.BlockSpec((None, d, EXPERT_TILE), lambda i, e: (layer, 0, e)),
            head_full, head_full, head_rows, head_rows,
            pl.BlockSpec(memory_space=pl.ANY),
            pl.BlockSpec((1, d), lambda i, e: (0, 0)),
        ],
        out_specs=pl.BlockSpec((bm, d), lambda i, e: (i, 0)),
        out_shape=jax.ShapeDtypeStruct((rows, d), F32),
        scratch_shapes=[
            pltpu.VMEM((EXPERT_TILE, bm), BF16),
            pltpu.VMEM((d, bm), F32),
            pltpu.SemaphoreType.DMA(()),
        ],
        compiler_params=pltpu.CompilerParams(dimension_semantics=("parallel", "arbitrary"),
                                             vmem_limit_bytes=EXPERT_VMEM_LIMIT),
        name="peer_experts",
    )(xnt, u, vt, r2, bz, cnt, a, x, g_final.reshape(1, d))


def _block_meta(groups, blk):
    pos, first, last, batch = [], [], [], []
    b0 = 0
    for (b, s) in groups:
        per = s // blk
        for bi in range(b):
            for j in range(per):
                pos.append(j)
                first.append(int(j == 0))
                last.append(int(j == per - 1))
                batch.append(b0 + bi)
        b0 += b
    return jnp.asarray(np.array([pos, first, last, batch], dtype=np.int32))


def _rope_table(max_pos):
    half = HEAD_DIM // 2
    inv = ROPE_THETA ** (-jnp.arange(half, dtype=F32) / half)
    ang = jnp.arange(max_pos, dtype=F32)[:, None] * inv[None, :]
    cos, sin = jnp.cos(ang), jnp.sin(ang)
    return jnp.concatenate([cos, cos, -sin, sin], axis=-1)


def _pick(t, pref):
    while t % pref:
        pref //= 2
    return pref


def _trunk(xs, mems, mix_norm, w_in, attn_sink, gm_ln_g, gm_ln_b, gm_w_s, gm_b_s, w_att_proj, w_gm_proj,
           w_out, cross_norm, mem_norm, w_q_mem, w_kv_mem, w_o_mem, peer_norm, w_q_peer, peer_k1, peer_k2,
           expert_u, expert_v, final_norm):
    groups = [(x.shape[0], x.shape[1]) for x in xs]
    depth = w_in.shape[0]
    for (_, s) in groups:
        assert s % ATT_BLOCK == 0
    x_parts = tuple(x.reshape(-1, D_MODEL) for x in xs)
    if len(x_parts) > 2:
        x_parts = (jnp.concatenate(x_parts, axis=0),)
    mem = jnp.concatenate([m.reshape(-1, D_MODEL) for m in mems], axis=0)
    n_mem_rows = mem.shape[0]
    meta = _block_meta(groups, ATT_BLOCK)
    cs_table = _rope_table(max(s for _, s in groups))

    common = int(np.gcd.reduce([b * s for b, s in groups]))
    bm_big = _pick(common, 1024)
    bm_mid = _pick(common, 512)
    bm_small = _pick(common, 256)
    u_all = expert_u.astype(BF16)
    vt_all = jnp.swapaxes(expert_v, 1, 2).astype(BF16)
    in_splits = np.cumsum([ATT_WIDTH, KV_WIDTH, KV_WIDTH, GM_WIDTH, GM_WIDTH, D_MODEL]).tolist()

    for l in range(depth):
        q_w, k_w, v_w, gu_w, gz_w, ga_w, gb_w = jnp.split(w_in[l], in_splits, axis=1)
        w_in_p = jnp.concatenate([q_w, gu_w, gz_w, ga_w, gb_w, k_w, v_w], axis=1).astype(BF16)
        sink_col = jnp.broadcast_to((attn_sink[l].astype(F32) * LOG2_E)[:, None, None],
                                    (N_Q_HEADS, WINDOW, SINK_LANES))
        b_s_b = jnp.broadcast_to(gm_b_s[l].astype(F32)[:, :, None],
                                 (GM_GROUPS, GM_CHUNK, GM_WIDTH // GM_GROUPS))

        proj = norm_matmul(x_parts, mix_norm[l], w_in_p, bm=bm_big, bn=1024)
        att = banded_attention(proj, meta, cs_table, sink_col)
        x = merge_project(att, proj, x_parts, gm_ln_g[l], gm_ln_b[l], gm_w_s[l].astype(BF16), b_s_b,
                          w_att_proj[l].astype(BF16), w_gm_proj[l].astype(BF16), w_out[l].astype(BF16),
                          bm=bm_small)

        kv = norm_matmul((mem,), mem_norm[l], w_kv_mem[l].astype(BF16), bm=_pick(n_mem_rows, 512), bn=1024)
        kv = kv.reshape(-1, MEM_TOKENS, 2 * MEM_WIDTH)
        x = cross_attention(x, meta, cross_norm[l], w_q_mem[l].astype(BF16), kv, w_o_mem[l].astype(BF16),
                            bm=ATT_BLOCK)

        xnt, r2, bz, cnt, a = peer_router(x, peer_norm[l], w_q_peer[l].T.astype(BF16),
                                          peer_k1[l].astype(BF16), peer_k2[l].astype(BF16), bm=bm_mid)
        experts = functools.partial(peer_experts, xnt, u_all, vt_all, r2, bz, cnt, a, x, final_norm,
                                    layer=l, bm=bm_mid)
        if l + 1 < depth:
            x_parts = (experts(),)
    outs = []
    start = 0
    for (b, s) in groups:
        y = experts(start=start, rows=b * s, final_norm=True)
        outs.append(y.reshape(b, s, D_MODEL))
        start += b * s
    return tuple(outs)


def kernel(x_prompt, x_sample, mem_prompt, mem_sample, mix_norm, w_in, attn_sink, gm_ln_g, gm_ln_b, gm_w_s, gm_b_s, w_att_proj, w_gm_proj, w_out, cross_norm, mem_norm, w_q_mem, w_kv_mem, w_o_mem, peer_norm, w_q_peer, peer_k1, peer_k2, expert_u, expert_v, final_norm):
    return _trunk((x_prompt, x_sample), (mem_prompt, mem_sample), mix_norm, w_in, attn_sink, gm_ln_g,
                  gm_ln_b, gm_w_s, gm_b_s, w_att_proj, w_gm_proj, w_out, cross_norm, mem_norm, w_q_mem,
                  w_kv_mem, w_o_mem, peer_norm, w_q_peer, peer_k1, peer_k2, expert_u, expert_v, final_norm)
```
